```python
import math
import jax, jax.numpy as jnp
from jax import lax
import numpy as np

D_MODEL = 4096
BATCH = 1
SEQ = 8192
DEPTH = 4

HEAD_DIM = 128
N_BRANCH = 4
BRANCH_WIDTH = D_MODEL // N_BRANCH
NSA_HEADS = BRANCH_WIDTH // HEAD_DIM
NSA_GROUPS = 2
NSA_REP = NSA_HEADS // NSA_GROUPS
CMP_BLOCK = 32
CMP_STRIDE = 16
SEL_BLOCK = 64
SEL_TOPN = 16
WINDOW = 512
Q_BLOCK = 128
DIFF_HEADS = BRANCH_WIDTH // (2 * HEAD_DIM)
SC_WIDTH = BRANCH_WIDTH
SC_CONV_W = 3
CF_WIDTH = BRANCH_WIDTH
CF_CONV_W = 31
D_FF = 7 * D_MODEL // 4
FFN_CONV_W = 3

N_ATTN_HEADS = NSA_HEADS + DIFF_HEADS
NORM_EPS = 1e-6
NEG_INF = -1e30

NSA_Q_COLS = NSA_HEADS * HEAD_DIM
NSA_KV_COLS = 3 * 2 * NSA_GROUPS * HEAD_DIM
NSA_GATE_COLS = 3 * NSA_HEADS
DIFF_QK_COLS = DIFF_HEADS * 2 * HEAD_DIM
DIFF_V_COLS = DIFF_HEADS * 2 * HEAD_DIM
SC_COLS = 3 * SC_WIDTH
CF_COLS = 2 * CF_WIDTH
MERGE_GATE_COLS = N_BRANCH * D_MODEL
IN_COLS = (NSA_Q_COLS + NSA_KV_COLS + NSA_GATE_COLS + 2 * DIFF_QK_COLS + DIFF_V_COLS
           + SC_COLS + CF_COLS + MERGE_GATE_COLS)

kernel_name = 'hybrid_nsa_diff_conv_trunk'


def _splits():
    sizes = [NSA_Q_COLS, NSA_KV_COLS, NSA_GATE_COLS, DIFF_QK_COLS, DIFF_QK_COLS,
             DIFF_V_COLS, SC_COLS, CF_COLS, MERGE_GATE_COLS]
    return [int(v) for v in np.cumsum(sizes)[:-1]]


def _rmsnorm(x, g):
    x32 = x.astype(jnp.float32)
    y = x32 * lax.rsqrt(jnp.mean(x32 * x32, axis=-1, keepdims=True) + NORM_EPS)
    return (y * g.astype(jnp.float32)).astype(x.dtype)


def _layernorm(x, g, b):
    x32 = x.astype(jnp.float32)
    mu = jnp.mean(x32, axis=-1, keepdims=True)
    xc = x32 - mu
    y = xc * lax.rsqrt(jnp.mean(xc * xc, axis=-1, keepdims=True) + NORM_EPS)
    return (y * g.astype(jnp.float32) + b.astype(jnp.float32)).astype(x.dtype)


def _causal_dwconv(x, w):
    k = w.shape[0]
    return lax.conv_general_dilated(
        x, w[:, None, :].astype(x.dtype), (1,), ((k - 1, 0),),
        dimension_numbers=('NWC', 'WIO', 'NWC'), feature_group_count=x.shape[-1])


def _alibi_slopes():
    s = 2.0 ** (-8.0 * np.arange(1, N_ATTN_HEADS + 1) / N_ATTN_HEADS)
    stride = N_ATTN_HEADS // DIFF_HEADS
    diff_idx = np.arange(DIFF_HEADS) * stride + stride - 1
    nsa_idx = np.setdiff1d(np.arange(N_ATTN_HEADS), diff_idx)
    nsa = jnp.asarray(s[nsa_idx].reshape(NSA_GROUPS, NSA_REP), dtype=jnp.float32)
    diff = jnp.asarray(s[diff_idx], dtype=jnp.float32)
    return nsa, diff


def _nsa(q, kc, vc, ks, vs, kw, vw, gates, cmp_pe, cmp_w1, cmp_w2, slopes):
    B, T = q.shape[:2]
    G, R, Dh = NSA_GROUPS, NSA_REP, HEAD_DIM
    scale = Dh ** -0.5
    pos = jnp.arange(T)

    n_cmp = (T - CMP_BLOCK) // CMP_STRIDE + 1
    cmp_start = np.arange(n_cmp) * CMP_STRIDE
    blk_idx = cmp_start[:, None] + np.arange(CMP_BLOCK)[None, :]

    def compress(t, pe, w1, w2):
        blocks = t[:, blk_idx] + pe[None, None, :, None, :]
        blocks = jnp.moveaxis(blocks, 3, 2).reshape(B, n_cmp, G, CMP_BLOCK * Dh)
        return jax.nn.gelu(blocks @ w1) @ w2

    k_cmp = compress(kc, cmp_pe[0], cmp_w1[0], cmp_w2[0])
    v_cmp = compress(vc, cmp_pe[1], cmp_w1[1], cmp_w2[1])
    cmp_end = jnp.asarray(cmp_start + CMP_BLOCK - 1)
    dist = pos[:, None] - cmp_end[None, :]
    vis = dist >= 0
    s = (jnp.einsum('btgrd,bngd->bgrtn', q, k_cmp).astype(jnp.float32) * scale
         - slopes[:, :, None, None] * dist.astype(jnp.float32))
    s = jnp.where(vis, s, NEG_INF)
    p_cmp = jax.nn.softmax(s, axis=-1) * vis.any(-1, keepdims=True).astype(jnp.float32)
    o_cmp = jnp.einsum('bgrtn,bngd->btgrd', p_cmp.astype(v_cmp.dtype), v_cmp)

    n_sel = T // SEL_BLOCK
    sel_start = np.arange(n_sel) * SEL_BLOCK
    shared_tokens = np.clip(
        np.minimum(cmp_start[:, None] + CMP_BLOCK, sel_start[None, :] + SEL_BLOCK)
        - np.maximum(cmp_start[:, None], sel_start[None, :]), 0, None).astype(np.float32)
    imp = jnp.einsum('bgrtn,nj->bgtj', p_cmp, jnp.asarray(shared_tokens))
    blk = jnp.arange(n_sel)[None, :]
    qblk = (pos // SEL_BLOCK)[:, None]
    forced = (blk == 0) | (blk == qblk) | (blk == qblk - 1)
    imp = jnp.where(forced, -NEG_INF, imp)
    imp = jnp.where(blk <= qblk, imp, NEG_INF)
    top_n = min(SEL_TOPN, n_sel)
    top_val, top_idx = lax.top_k(imp, top_n)
    top_ok = top_val > 0.5 * NEG_INF

    n_q = T // Q_BLOCK
    ks_blk = ks.reshape(B, n_sel, SEL_BLOCK, G, Dh).transpose(0, 3, 1, 2, 4)
    vs_blk = vs.reshape(B, n_sel, SEL_BLOCK, G, Dh).transpose(0, 3, 1, 2, 4)
    q_chunks = q.reshape(B, n_q, Q_BLOCK, G, R, Dh).transpose(1, 0, 2, 3, 4, 5)
    idx_chunks = top_idx.reshape(B, G, n_q, Q_BLOCK, top_n).transpose(2, 0, 1, 3, 4)
    ok_chunks = top_ok.reshape(B, G, n_q, Q_BLOCK, top_n).transpose(2, 0, 1, 3, 4)
    gather = jax.vmap(jax.vmap(lambda blocks, ix: blocks[ix]))

    def sel_chunk(args):
        c, qc, ic, oc = args
        kg = gather(ks_blk, ic)
        vg = gather(vs_blk, ic)
        tq = c * Q_BLOCK + jnp.arange(Q_BLOCK)
        kpos = ic[..., None] * SEL_BLOCK + jnp.arange(SEL_BLOCK)
        d = tq[None, None, :, None, None] - kpos
        m = (d >= 0) & oc[..., None]
        sc = (jnp.einsum('bqgrd,bgqksd->bgrqks', qc, kg).astype(jnp.float32) * scale
              - slopes[None, :, :, None, None, None] * d[:, :, None].astype(jnp.float32))
        sc = jnp.where(m[:, :, None], sc, NEG_INF)
        p = jax.nn.softmax(sc, axis=(-2, -1))
        return jnp.einsum('bgrqks,bgqksd->bqgrd', p.astype(vg.dtype), vg)

    o_sel = lax.map(sel_chunk, (jnp.arange(n_q), q_chunks, idx_chunks, ok_chunks))
    o_sel = o_sel.transpose(1, 0, 2, 3, 4, 5).reshape(B, T, G, R, Dh)

    nw = WINDOW // Q_BLOCK

    def neighbourhood(t):
        tp = jnp.pad(t, ((0, 0), (nw * Q_BLOCK, 0), (0, 0), (0, 0)))
        tp = tp.reshape(B, n_q + nw, Q_BLOCK, G, Dh)
        return jnp.stack([tp[:, i:i + n_q] for i in range(nw + 1)], axis=2)

    kn = neighbourhood(kw)
    vn = neighbourhood(vw)
    qb = q.reshape(B, n_q, Q_BLOCK, G, R, Dh)
    qpos = jnp.arange(n_q)[:, None] * Q_BLOCK + jnp.arange(Q_BLOCK)[None, :]
    kpos = ((jnp.arange(n_q)[:, None, None] + jnp.arange(nw + 1)[None, :, None] - nw) * Q_BLOCK
            + jnp.arange(Q_BLOCK)[None, None, :])
    d = qpos[:, :, None, None] - kpos[:, None]
    m = (d >= 0) & (d < WINDOW) & (kpos[:, None] >= 0)
    sw = (jnp.einsum('bnqgrd,bnwsgd->bgrnqws', qb, kn).astype(jnp.float32) * scale
          - slopes[:, :, None, None, None, None] * d.astype(jnp.float32))
    sw = jnp.where(m, sw, NEG_INF)
    pw = jax.nn.softmax(sw, axis=(-2, -1))
    o_win = jnp.einsum('bgrnqws,bnwsgd->bnqgrd', pw.astype(vn.dtype), vn).reshape(B, T, G, R, Dh)

    o = gates[..., 0:1] * o_cmp + gates[..., 1:2] * o_sel + gates[..., 2:3] * o_win
    return o.reshape(B, T, G * R * Dh)


def _diff_attention(q, k, v, lam, subln_g, slopes, lam_init):
    B, T, H = q.shape[:3]
    n_q = T // Q_BLOCK
    scale = HEAD_DIM ** -0.5
    kpos = jnp.arange(T)
    q_chunks = q.reshape(B, n_q, Q_BLOCK, H, 2, HEAD_DIM).swapaxes(0, 1)

    def block(args):
        c, qc = args
        dist = (c * Q_BLOCK + jnp.arange(Q_BLOCK))[:, None] - kpos[None, :]
        s = (jnp.einsum('bqhcd,bkhcd->bhcqk', qc, k).astype(jnp.float32) * scale
             - slopes[:, None, None, None] * dist.astype(jnp.float32))
        s = jnp.where(dist >= 0, s, NEG_INF)
        p = jax.nn.softmax(s, axis=-1)
        a = p[:, :, 0] - lam * p[:, :, 1]
        return jnp.einsum('bhqk,bkhe->bqhe', a.astype(v.dtype), v)

    o = lax.map(block, (jnp.arange(n_q), q_chunks))
    o = o.swapaxes(0, 1).reshape(B, T, H, 2 * HEAD_DIM)
    o = _rmsnorm(o, subln_g) * (1.0 - lam_init)
    return o.reshape(B, T, H * 2 * HEAD_DIM)


def setup_inputs(seed: int = 0) -> dict:
    key = jax.random.key(seed)
    ks = jax.random.split(key, 20)
    f32 = jnp.float32

    def nrm(k, shape, scale):
        return jax.random.normal(k, shape, f32) * scale

    return {
        'x': nrm(ks[0], (BATCH, SEQ, D_MODEL), 1.0),
        'attn_norm': 1.0 + nrm(ks[1], (DEPTH, D_MODEL), 0.02),
        'w_in': nrm(ks[2], (DEPTH, D_MODEL, IN_COLS), D_MODEL ** -0.5),
        'cmp_pe': nrm(ks[3], (DEPTH, 2, CMP_BLOCK, HEAD_DIM), 0.1),
        'cmp_w1': nrm(ks[4], (DEPTH, 2, CMP_BLOCK * HEAD_DIM, HEAD_DIM), (CMP_BLOCK * HEAD_DIM) ** -0.5),
        'cmp_w2': nrm(ks[5], (DEPTH, 2, HEAD_DIM, HEAD_DIM), HEAD_DIM ** -0.5),
        'diff_lambda': nrm(ks[6], (DEPTH, 4, HEAD_DIM), 0.1),
        'diff_subln': 1.0 + nrm(ks[7], (DEPTH, 2 * HEAD_DIM), 0.02),
        'sc_conv': nrm(ks[8], (DEPTH, SC_CONV_W, SC_WIDTH), SC_CONV_W ** -0.5),
        'cf_conv_w': nrm(ks[9], (DEPTH, CF_CONV_W, CF_WIDTH), CF_CONV_W ** -0.5),
        'cf_conv_b': nrm(ks[10], (DEPTH, CF_WIDTH), 0.02),
        'cf_ln_g': 1.0 + nrm(ks[11], (DEPTH, CF_WIDTH), 0.02),
        'cf_ln_b': nrm(ks[12], (DEPTH, CF_WIDTH), 0.02),
        'w_branch': nrm(ks[13], (DEPTH, N_BRANCH, BRANCH_WIDTH, D_MODEL), BRANCH_WIDTH ** -0.5),
        'w_out': nrm(ks[14], (DEPTH, D_MODEL, D_MODEL), D_MODEL ** -0.5),
        'ffn_norm': 1.0 + nrm(ks[15], (DEPTH, D_MODEL), 0.02),
        'w_up': nrm(ks[16], (DEPTH, D_MODEL, 2 * D_FF), D_MODEL ** -0.5),
        'ffn_conv': nrm(ks[17], (DEPTH, FFN_CONV_W, 2 * D_FF), FFN_CONV_W ** -0.5),
        'w_down': nrm(ks[18], (DEPTH, D_FF, D_MODEL), D_FF ** -0.5),
        'final_norm': 1.0 + nrm(ks[19], (D_MODEL,), 0.02),
    }


def reference(x, attn_norm, w_in, cmp_pe, cmp_w1, cmp_w2, diff_lambda, diff_subln, sc_conv,
              cf_conv_w, cf_conv_b, cf_ln_g, cf_ln_b, w_branch, w_out, ffn_norm, w_up,
              ffn_conv, w_down, final_norm):
    B, T, _ = x.shape
    slopes_nsa, slopes_diff = _alibi_slopes()
    for l in range(DEPTH):
        h = _rmsnorm(x, attn_norm[l])
        proj = h @ w_in[l]
        nsa_q, nsa_kv, nsa_g, d_q, d_k, d_v, sc_in, cf_in, merge_in = jnp.split(proj, _splits(), axis=-1)

        q = nsa_q.reshape(B, T, NSA_GROUPS, NSA_REP, HEAD_DIM)
        kv = nsa_kv.reshape(B, T, 3, 2, NSA_GROUPS, HEAD_DIM)
        g_nsa = jax.nn.sigmoid(nsa_g.reshape(B, T, NSA_GROUPS, NSA_REP, 3))
        y_a = _nsa(q, kv[:, :, 0, 0], kv[:, :, 0, 1], kv[:, :, 1, 0], kv[:, :, 1, 1],
                   kv[:, :, 2, 0], kv[:, :, 2, 1], g_nsa, cmp_pe[l], cmp_w1[l], cmp_w2[l], slopes_nsa)

        lam_init = 0.8 - 0.6 * math.exp(-0.3 * l)
        lv = diff_lambda[l].astype(jnp.float32)
        lam = jnp.exp(jnp.sum(lv[0] * lv[1])) - jnp.exp(jnp.sum(lv[2] * lv[3])) + lam_init
        y_b = _diff_attention(d_q.reshape(B, T, DIFF_HEADS, 2, HEAD_DIM),
                              d_k.reshape(B, T, DIFF_HEADS, 2, HEAD_DIM),
                              d_v.reshape(B, T, DIFF_HEADS, 2 * HEAD_DIM),
                              lam, diff_subln[l], slopes_diff, lam_init)

        sc_b, sc_c, sc_x = jnp.split(sc_in, 3, axis=-1)
        y_c = sc_b * _causal_dwconv(sc_c * sc_x, sc_conv[l])

        cf_a, cf_g = jnp.split(cf_in, 2, axis=-1)
        u = cf_a * jax.nn.sigmoid(cf_g)
        u = _causal_dwconv(u, cf_conv_w[l]) + cf_conv_b[l]
        y_d = jax.nn.silu(_layernorm(u, cf_ln_g[l], cf_ln_b[l]))

        gates = jax.nn.sigmoid(merge_in.reshape(B, T, N_BRANCH, D_MODEL))
        branches = (y_a, y_b, y_c, y_d)
        merged = gates[:, :, 0] * (branches[0] @ w_branch[l, 0])
        for i in range(1, N_BRANCH):
            merged = merged + gates[:, :, i] * (branches[i] @ w_branch[l, i])
        x = x + merged @ w_out[l]

        h = _rmsnorm(x, ffn_norm[l])
        up = _causal_dwconv(h @ w_up[l], ffn_conv[l])
        gate, val = jnp.split(up, 2, axis=-1)
        x = x + (jax.nn.silu(gate) * val) @ w_down[l]
    return _rmsnorm(x, final_norm)
```

```python
import functools
import math

import numpy as np
import jax
import jax.numpy as jnp
from jax import lax
from jax.experimental import pallas as pl
from jax.experimental.pallas import tpu as pltpu

F32 = jnp.float32
BF16 = jnp.bfloat16

HEAD_DIM = 128
NSA_GROUPS = 2
CMP_BLOCK = 32
CMP_STRIDE = 16
SEL_BLOCK = 64
SEL_BLOCK_LOG2 = 6
SEL_TOPN = 16
WINDOW = 512
SC_CONV_W = 3
CF_CONV_W = 31
FFN_CONV_W = 3
NORM_EPS = 1e-6
NEG = -1e30
LANES = 128
BF16_SUBLANES = 16
VMEM_LIMIT_BYTES = 56 * 1024 * 1024


def _cparams(sem):
    return pltpu.CompilerParams(dimension_semantics=sem, vmem_limit_bytes=VMEM_LIMIT_BYTES)


def _tile(dim, pref):
    t = min(pref, dim)
    while dim % t:
        t //= 2
    return t


def _sigmoid(x):
    return 1.0 / (1.0 + jnp.exp(-x))


def _dot(a, b):
    return jnp.dot(a, b, preferred_element_type=F32)


def _dot_nt(a, b):
    return lax.dot_general(a, b, (((1,), (1,)), ((), ())), preferred_element_type=F32)


def _rmsnorm_kernel(x_ref, g_ref, o_ref):
    x = x_ref[...]
    y = x * lax.rsqrt(jnp.mean(x * x, axis=-1, keepdims=True) + NORM_EPS)
    o_ref[...] = (y * g_ref[...]).astype(o_ref.dtype)


def _rmsnorm(x, g, out_dtype):
    t, d = x.shape
    tm = _tile(t, 256)
    return pl.pallas_call(
        _rmsnorm_kernel,
        grid=(t // tm,),
        in_specs=[pl.BlockSpec((tm, d), lambda i: (i, 0)), pl.BlockSpec((1, d), lambda i: (0, 0))],
        out_specs=pl.BlockSpec((tm, d), lambda i: (i, 0)),
        out_shape=jax.ShapeDtypeStruct((t, d), out_dtype),
        compiler_params=_cparams(("parallel",)),
        name="rmsnorm",
    )(x, g.reshape(1, d).astype(F32))


def _mm_kernel(*refs, act, has_res, nk):
    a_ref, w_ref = refs[0], refs[1]
    res_ref = refs[2] if has_res else None
    o_ref = refs[3] if has_res else refs[2]
    part = _dot(a_ref[...], w_ref[...])

    def finish(acc):
        if act == "sigmoid":
            acc = _sigmoid(acc)
        if has_res:
            acc = acc + res_ref[...]
        o_ref[...] = acc.astype(o_ref.dtype)

    if nk == 1:
        finish(part)
    else:
        acc_ref = refs[-1]
        k = pl.program_id(2)

        @pl.when(k == 0)
        def _():
            acc_ref[...] = part

        @pl.when(k > 0)
        def _():
            acc_ref[...] += part

        @pl.when(k == nk - 1)
        def _():
            finish(acc_ref[...])


def _matmul(a, w, *, out_dtype, act=None, res=None, tm=1024, tn=1024, tk=4096, name="matmul"):
    m, kd = a.shape
    n = w.shape[1]
    tm, tn, tk = _tile(m, tm), _tile(n, tn), _tile(kd, tk)
    nk = kd // tk
    in_specs = [pl.BlockSpec((tm, tk), lambda i, j, k: (i, k)),
                pl.BlockSpec((tk, tn), lambda i, j, k: (k, j))]
    args = [a, w]
    if res is not None:
        in_specs.append(pl.BlockSpec((tm, tn), lambda i, j, k: (i, j)))
        args.append(res)
    return pl.pallas_call(
        functools.partial(_mm_kernel, act=act, has_res=res is not None, nk=nk),
        grid=(m // tm, n // tn, nk),
        in_specs=in_specs,
        out_specs=pl.BlockSpec((tm, tn), lambda i, j, k: (i, j)),
        out_shape=jax.ShapeDtypeStruct((m, n), out_dtype),
        scratch_shapes=[pltpu.VMEM((tm, tn), F32)] if nk > 1 else [],
        compiler_params=_cparams(("parallel", "parallel", "arbitrary")),
        name=name,
    )(*args)


def _gelu_tanh(x):
    return 0.5 * x * (1.0 + jnp.tanh(math.sqrt(2.0 / math.pi) * (x + 0.044715 * (x * x * x))))


def _compress_kernel(r_ref, pe_ref, w1_ref, w2_ref, o_ref, *, ncp, half):
    r = r_ref[...]
    u = _dot(r, w1_ref[:half, :])
    v = _dot(r, w1_ref[half:, :])
    c = _dot(pe_ref[...], w1_ref[...])[0:1]
    pre = u + pltpu.roll(v, ncp - 1, 0) + c
    out = _dot(_gelu_tanh(pre).astype(BF16), w2_ref[...])
    row = lax.broadcasted_iota(jnp.int32, out.shape, 0)
    o_ref[...] = jnp.where(row < ncp - 1, out, 0.0).astype(o_ref.dtype)


def _compress(chunks, pe, w1, w2):
    _, g, ncp, half = chunks.shape
    return pl.pallas_call(
        functools.partial(_compress_kernel, ncp=ncp, half=half),
        grid=(2, g),
        in_specs=[pl.BlockSpec((None, None, ncp, half), lambda a, b: (a, b, 0, 0)),
                  pl.BlockSpec((None, BF16_SUBLANES, 2 * half), lambda a, b: (a, 0, 0)),
                  pl.BlockSpec((None, 2 * half, HEAD_DIM), lambda a, b: (a, 0, 0)),
                  pl.BlockSpec((None, HEAD_DIM, HEAD_DIM), lambda a, b: (a, 0, 0))],
        out_specs=pl.BlockSpec((None, None, ncp, HEAD_DIM), lambda a, b: (a, b, 0, 0)),
        out_shape=jax.ShapeDtypeStruct((2, g, ncp, HEAD_DIM), BF16),
        compiler_params=_cparams(("parallel", "parallel")),
        name="nsa_compress",
    )(chunks, pe, w1, w2)


def _cmp_attn_kernel(slopes_ref, q_ref, k_ref, v_ref, shared_ref, o_ref, imp_ref, *, tq, rep, scale):
    g, i = pl.program_id(0), pl.program_id(1)
    kc, vc = k_ref[...], v_ref[...]
    ncp = kc.shape[0]
    tpos = i * tq + lax.broadcasted_iota(jnp.int32, (tq, ncp), 0)
    cend = lax.broadcasted_iota(jnp.int32, (tq, ncp), 1) * CMP_STRIDE + (CMP_BLOCK - 1)
    dist = tpos - cend
    vis = dist >= 0
    distf = dist.astype(F32)
    anyvis = tpos >= CMP_BLOCK - 1
    psum = jnp.zeros((tq, ncp), F32)
    for r in range(rep):
        slope = slopes_ref[g * rep + r]
        s = _dot_nt(q_ref[:, r * HEAD_DIM:(r + 1) * HEAD_DIM], kc) * scale - slope * distf
        s = jnp.where(vis, s, NEG)
        p = jnp.exp(s - jnp.max(s, axis=-1, keepdims=True))
        p = jnp.where(anyvis, p / jnp.sum(p, axis=-1, keepdims=True), 0.0)
        o_ref[:, r * HEAD_DIM:(r + 1) * HEAD_DIM] = _dot(p.astype(BF16), vc)
        psum = psum + p
    hi = psum.astype(BF16)
    lo = (psum - hi.astype(F32)).astype(BF16)
    imp_ref[...] = _dot(hi, shared_ref[...]) + _dot(lo, shared_ref[...])


def _cmp_attn(slopes, qkv, kv_cmp, shared, *, t, g, rep, tq):
    ncp, nselp = shared.shape
    hw = rep * HEAD_DIM
    return pl.pallas_call(
        functools.partial(_cmp_attn_kernel, tq=tq, rep=rep, scale=HEAD_DIM ** -0.5),
        grid=(g, t // tq),
        in_specs=[pl.BlockSpec(memory_space=pltpu.SMEM),
                  pl.BlockSpec((tq, hw), lambda a, i: (i, a)),
                  pl.BlockSpec((None, None, ncp, HEAD_DIM), lambda a, i: (0, a, 0, 0)),
                  pl.BlockSpec((None, None, ncp, HEAD_DIM), lambda a, i: (1, a, 0, 0)),
                  pl.BlockSpec((ncp, nselp), lambda a, i: (0, 0))],
        out_specs=[pl.BlockSpec((tq, hw), lambda a, i: (i, a)),
                   pl.BlockSpec((None, tq, nselp), lambda a, i: (a, i, 0))],
        out_shape=[jax.ShapeDtypeStruct((t, g * hw), F32),
                   jax.ShapeDtypeStruct((g, t, nselp), F32)],
        compiler_params=_cparams(("parallel", "parallel")),
        name="nsa_cmp_attn",
    )(slopes, qkv, kv_cmp, kv_cmp, shared)


def _topk_kernel(imp_ref, sel_ref, *, tq, top_n):
    i = pl.program_id(1)
    imp = imp_ref[...]
    blk = lax.broadcasted_iota(jnp.int32, imp.shape, 1)
    qblk = jnp.right_shift(i * tq + lax.broadcasted_iota(jnp.int32, imp.shape, 0), SEL_BLOCK_LOG2)
    forced = (blk == 0) | (blk == qblk) | (blk == qblk - 1)
    vals = jnp.where(forced, -NEG, imp)
    vals = jnp.where(blk <= qblk, vals, NEG)
    blkf = blk.astype(F32)
    nlanes = float(imp.shape[1])

    def body(_, carry):
        vals, sel = carry
        m = jnp.max(vals, axis=-1, keepdims=True)
        first = jnp.min(jnp.where(vals == m, blkf, nlanes), axis=-1, keepdims=True)
        pick = blkf == first
        sel = jnp.where(pick & (m > 0.5 * NEG), 1.0, sel)
        return jnp.where(pick, -jnp.inf, vals), sel

    _, sel = lax.fori_loop(0, top_n, body, (vals, jnp.zeros(imp.shape, F32)))
    sel_ref[...] = sel.astype(sel_ref.dtype)


def _topk_mask(imp, *, top_n):
    g, t, nselp = imp.shape
    tq = _tile(t, 128)
    return pl.pallas_call(
        functools.partial(_topk_kernel, tq=tq, top_n=top_n),
        grid=(g, t // tq),
        in_specs=[pl.BlockSpec((None, tq, nselp), lambda a, i: (a, i, 0))],
        out_specs=pl.BlockSpec((None, tq, nselp), lambda a, i: (a, i, 0)),
        out_shape=jax.ShapeDtypeStruct((g, t, nselp), BF16),
        compiler_params=_cparams(("parallel", "parallel")),
        name="nsa_topk",
    )(imp)


def _online_update(s, valid, vt, m_ref, l_ref, acc_ref, idx):
    m_old = m_ref[idx]
    m_new = jnp.maximum(m_old, jnp.max(s, axis=-1, keepdims=True))
    p = jnp.exp(s - m_new)
    if valid is not None:
        p = jnp.where(valid, p, 0.0)
    alpha = jnp.exp(m_old - m_new)
    l_ref[idx] = alpha * l_ref[idx] + jnp.sum(p, axis=-1, keepdims=True)
    acc_ref[idx] = alpha * acc_ref[idx] + _dot(p.astype(BF16), vt)
    m_ref[idx] = m_new


def _sel_attn_kernel(slopes_ref, q_ref, k_ref, v_ref, sel_ref, o_ref, m_ref, l_ref, acc_ref, *, tq, rep, scale):
    g, i = pl.program_id(0), pl.program_id(1)
    tk = tq
    m_ref[...] = jnp.full(m_ref.shape, NEG, F32)
    l_ref[...] = jnp.zeros(l_ref.shape, F32)
    acc_ref[...] = jnp.zeros(acc_ref.shape, F32)
    sel = sel_ref[...]
    nselp = sel.shape[1]
    blk_row = lax.broadcasted_iota(jnp.int32, (nselp, tk), 0)
    blk_col = jnp.right_shift(lax.broadcasted_iota(jnp.int32, (nselp, tk), 1), SEL_BLOCK_LOG2)
    row = lax.broadcasted_iota(jnp.int32, (tq, tk), 0)
    col = lax.broadcasted_iota(jnp.int32, (tq, tk), 1)
    col1 = lax.broadcasted_iota(jnp.int32, (1, tk), 1)

    def step(k, carry):
        start = pl.multiple_of(k * tk, tk)
        kt = k_ref[pl.ds(start, tk), :]
        vt = v_ref[pl.ds(start, tk), :]
        expand = (blk_row == blk_col + k * (tk // SEL_BLOCK)).astype(BF16)
        chosen = _dot(sel, expand) > 0.5
        valid = chosen & (row + (i - k) * tq >= col)
        rel = (col1 + (k - i) * tq).astype(F32)
        for r in range(rep):
            slope = slopes_ref[g * rep + r]
            s = _dot_nt(q_ref[:, r * HEAD_DIM:(r + 1) * HEAD_DIM], kt) * scale + slope * rel
            s = jnp.where(valid, s, NEG)
            _online_update(s, valid, vt, m_ref, l_ref, acc_ref, r)
        return carry

    lax.fori_loop(0, i + 1, step, 0)
    for r in range(rep):
        o_ref[:, r * HEAD_DIM:(r + 1) * HEAD_DIM] = acc_ref[r] / l_ref[r]


def _sel_attn(slopes, qkv, sel, *, t, g, rep, tq, k_blk, v_blk):
    nselp = sel.shape[2]
    hw = rep * HEAD_DIM
    return pl.pallas_call(
        functools.partial(_sel_attn_kernel, tq=tq, rep=rep, scale=HEAD_DIM ** -0.5),
        grid=(g, t // tq),
        in_specs=[pl.BlockSpec(memory_space=pltpu.SMEM),
                  pl.BlockSpec((tq, hw), lambda a, i: (i, a)),
                  pl.BlockSpec((t, HEAD_DIM), lambda a, i: (0, k_blk + a)),
                  pl.BlockSpec((t, HEAD_DIM), lambda a, i: (0, v_blk + a)),
                  pl.BlockSpec((None, tq, nselp), lambda a, i: (a, i, 0))],
        out_specs=pl.BlockSpec((tq, hw), lambda a, i: (i, a)),
        out_shape=jax.ShapeDtypeStruct((t, g * hw), F32),
        scratch_shapes=[pltpu.VMEM((rep, tq, 1), F32), pltpu.VMEM((rep, tq, 1), F32),
                        pltpu.VMEM((rep, tq, HEAD_DIM), F32)],
        compiler_params=_cparams(("parallel", "parallel")),
        name="nsa_sel_attn",
    )(slopes, qkv, qkv, qkv, sel)


def _win_attn_kernel(slopes_ref, q_ref, k_ref, v_ref, gate_ref, ocmp_ref, osel_ref, o_ref, *, tq, rep, scale):
    g, i = pl.program_id(0), pl.program_id(1)
    span = tq + WINDOW
    start = pl.multiple_of(jnp.maximum(i * tq - WINDOW, 0), tq)
    kt = k_ref[pl.ds(start, span), :]
    vt = v_ref[pl.ds(start, span), :]
    tpos = i * tq + lax.broadcasted_iota(jnp.int32, (tq, span), 0)
    kpos = start + lax.broadcasted_iota(jnp.int32, (tq, span), 1)
    dist = tpos - kpos
    valid = (dist >= 0) & (dist < WINDOW)
    distf = dist.astype(F32)
    gates = _sigmoid(gate_ref[...])
    lane = lax.broadcasted_iota(jnp.int32, gates.shape, 1)

    def gate_col(c):
        return jnp.sum(jnp.where(lane == c, gates, 0.0), axis=-1, keepdims=True)

    for r in range(rep):
        head = g * rep + r
        s = _dot_nt(q_ref[:, r * HEAD_DIM:(r + 1) * HEAD_DIM], kt) * scale - slopes_ref[head] * distf
        s = jnp.where(valid, s, NEG)
        p = jnp.exp(s - jnp.max(s, axis=-1, keepdims=True))
        p = p / jnp.sum(p, axis=-1, keepdims=True)
        o_win = _dot(p.astype(BF16), vt)
        cols = slice(r * HEAD_DIM, (r + 1) * HEAD_DIM)
        o = (gate_col(3 * head) * ocmp_ref[:, cols] + gate_col(3 * head + 1) * osel_ref[:, cols]
             + gate_col(3 * head + 2) * o_win)
        o_ref[:, cols] = o.astype(o_ref.dtype)


def _win_attn(slopes, qkv, gate_logits, o_cmp, o_sel, *, t, g, rep, tq, k_blk, v_blk):
    hw = rep * HEAD_DIM
    return pl.pallas_call(
        functools.partial(_win_attn_kernel, tq=tq, rep=rep, scale=HEAD_DIM ** -0.5),
        grid=(g, t // tq),
        in_specs=[pl.BlockSpec(memory_space=pltpu.SMEM),
                  pl.BlockSpec((tq, hw), lambda a, i: (i, a)),
                  pl.BlockSpec((t, HEAD_DIM), lambda a, i: (0, k_blk + a)),
                  pl.BlockSpec((t, HEAD_DIM), lambda a, i: (0, v_blk + a)),
                  pl.BlockSpec((tq, LANES), lambda a, i: (i, 0)),
                  pl.BlockSpec((tq, hw), lambda a, i: (i, a)),
                  pl.BlockSpec((tq, hw), lambda a, i: (i, a))],
        out_specs=pl.BlockSpec((tq, hw), lambda a, i: (i, a)),
        out_shape=jax.ShapeDtypeStruct((t, g * hw), BF16),
        compiler_params=_cparams(("parallel", "parallel")),
        name="nsa_win_attn",
    )(slopes, qkv, qkv, qkv, gate_logits, o_cmp, o_sel)


def _diff_attn_kernel(slopes_ref, lam_init_ref, q_ref, k_ref, v_ref, lam_ref, g_ref, o_ref,
                      m_ref, l_ref, acc_ref, *, tq, scale):
    h, i = pl.program_id(0), pl.program_id(1)
    tk = tq
    slope = slopes_ref[h]
    m_ref[...] = jnp.full(m_ref.shape, NEG, F32)
    l_ref[...] = jnp.zeros(l_ref.shape, F32)
    acc_ref[...] = jnp.zeros(acc_ref.shape, F32)
    col1 = lax.broadcasted_iota(jnp.int32, (1, tk), 1)
    causal =lax.broadcasted_iota(jnp.int32, (tq, tk), 0) >= lax.broadcasted_iota(jnp.int32, (tq, tk), 1)

    def step(k, valid):
        start = pl.multiple_of(k * tk, tk)
        kt = k_ref[pl.ds(start, tk), :]
        vt = v_ref[pl.ds(start, tk), :]
        bias = slope * (col1 + (k - i) * tq).astype(F32)
        for c in range(2):
            cols = slice(c * HEAD_DIM, (c + 1) * HEAD_DIM)
            s = _dot_nt(q_ref[:, cols], kt[:, cols]) * scale + bias
            if valid is not None:
                s = jnp.where(valid, s, NEG)
            _online_update(s, valid, vt, m_ref, l_ref, acc_ref, c)

    def full_step(k, carry):
        step(k, None)
        return carry

    lax.fori_loop(0, i, full_step, 0)
    step(i, causal)

    lv = lam_ref[...]
    lam_init = lam_init_ref[0]
    lam = (jnp.exp(jnp.sum(lv[0:1] * lv[1:2], axis=-1, keepdims=True))
           - jnp.exp(jnp.sum(lv[2:3] * lv[3:4], axis=-1, keepdims=True)) + lam_init)
    o = acc_ref[0] / l_ref[0] - lam * (acc_ref[1] / l_ref[1])
    o = o * lax.rsqrt(jnp.mean(o * o, axis=-1, keepdims=True) + NORM_EPS) * g_ref[...]
    o_ref[...] = (o * (1.0 - lam_init)).astype(o_ref.dtype)


def _diff_attn(slopes, lam_init, proj, lam_vec, subln_g, *, t, heads, tq, q_blk, k_blk, v_blk):
    hw = 2 * HEAD_DIM
    return pl.pallas_call(
        functools.partial(_diff_attn_kernel, tq=tq, scale=HEAD_DIM ** -0.5),
        grid=(heads, t // tq),
        in_specs=[pl.BlockSpec(memory_space=pltpu.SMEM),
                  pl.BlockSpec(memory_space=pltpu.SMEM),
                  pl.BlockSpec((tq, hw), lambda h, i: (i, q_blk + h)),
                  pl.BlockSpec((t, hw), lambda h, i: (0, k_blk + h)),
                  pl.BlockSpec((t, hw), lambda h, i: (0, v_blk + h)),
                  pl.BlockSpec((4, HEAD_DIM), lambda h, i: (0, 0)),
                  pl.BlockSpec((1, hw), lambda h, i: (0, 0))],
        out_specs=pl.BlockSpec((tq, hw), lambda h, i: (i, h)),
        out_shape=jax.ShapeDtypeStruct((t, heads * hw), BF16),
        scratch_shapes=[pltpu.VMEM((2, tq, 1), F32), pltpu.VMEM((2, tq, 1), F32),
                        pltpu.VMEM((2, tq, hw), F32)],
        compiler_params=_cparams(("parallel", "parallel")),
        name="diff_attn",
    )(slopes, lam_init, proj, proj, proj, lam_vec, subln_g)


def _fill_conv_scratch(scr_ref, halo, cur, halo_rows, first_tile):
    scr_ref[0:halo_rows, :] = jnp.where(first_tile, 0.0, halo)
    scr_ref[halo_rows:, :] = cur


def _causal_taps(scr_ref, w_ref, halo_rows, rows, taps):
    acc = None
    for k in range(taps):
        term = w_ref[k:k + 1, :] * scr_ref[pl.ds(halo_rows - taps + 1 + k, rows), :]
        acc = term if acc is None else acc + term
    return acc


def _halo_spec(rows, halo_rows, width, col_blk):
    per = rows // halo_rows
    return pl.BlockSpec((halo_rows, width), lambda i, j: (jnp.maximum(i * per - 1, 0), col_blk + j))


def _sconv_kernel(b_ref, c_ref, x_ref, ch_ref, xh_ref, w_ref, o_ref, scr_ref, *, tm, halo_rows):
    first = pl.program_id(0) == 0
    halo = ch_ref[...].astype(F32) * xh_ref[...].astype(F32)
    cur = c_ref[...].astype(F32) * x_ref[...].astype(F32)
    _fill_conv_scratch(scr_ref, halo, cur, halo_rows, first)
    y = b_ref[...].astype(F32) * _causal_taps(scr_ref, w_ref, halo_rows, tm, SC_CONV_W)
    o_ref[...] = y.astype(o_ref.dtype)


def _sconv(proj, w, *, t, width, b_blk, tc):
    tm = _tile(t, 512)
    hr = BF16_SUBLANES
    nb = width // tc
    cur = lambda off: pl.BlockSpec((tm, tc), lambda i, j: (i, off + j))
    return pl.pallas_call(
        functools.partial(_sconv_kernel, tm=tm, halo_rows=hr),
        grid=(t // tm, nb),
        in_specs=[cur(b_blk), cur(b_blk + nb), cur(b_blk + 2 * nb),
                  _halo_spec(tm, hr, tc, b_blk + nb), _halo_spec(tm, hr, tc, b_blk + 2 * nb),
                  pl.BlockSpec((SC_CONV_W, tc), lambda i, j: (0, j))],
        out_specs=pl.BlockSpec((tm, tc), lambda i, j: (i, j)),
        out_shape=jax.ShapeDtypeStruct((t, width), BF16),
        scratch_shapes=[pltpu.VMEM((hr + tm, tc), F32)],
        compiler_params=_cparams(("parallel", "parallel")),
        name="short_conv",
    )(proj, proj, proj, proj, proj, w)


def _conformer_kernel(a_ref, g_ref, ah_ref, gh_ref, w_ref, b_ref, lg_ref, lb_ref, o_ref, scr_ref, *, tm, halo_rows):
    first = pl.program_id(0) == 0
    halo = ah_ref[...].astype(F32) * _sigmoid(gh_ref[...].astype(F32))
    cur = a_ref[...].astype(F32) * _sigmoid(g_ref[...].astype(F32))
    _fill_conv_scratch(scr_ref, halo, cur, halo_rows, first)
    u = _causal_taps(scr_ref, w_ref, halo_rows, tm, CF_CONV_W) + b_ref[...]
    mu = jnp.mean(u, axis=-1, keepdims=True)
    uc = u - mu
    y = uc * lax.rsqrt(jnp.mean(uc * uc, axis=-1, keepdims=True) + NORM_EPS)
    y = y * lg_ref[...] + lb_ref[...]
    o_ref[...] = (y * _sigmoid(y)).astype(o_ref.dtype)


def _conformer(proj, w, b, ln_g, ln_b, *, t, width, a_blk):
    tm = _tile(t, 256)
    hr = 32
    vec = pl.BlockSpec((1, width), lambda i, j: (0, 0))
    return pl.pallas_call(
        functools.partial(_conformer_kernel, tm=tm, halo_rows=hr),
        grid=(t // tm, 1),
        in_specs=[pl.BlockSpec((tm, width), lambda i, j: (i, a_blk)),
                  pl.BlockSpec((tm, width), lambda i, j: (i, a_blk + 1)),
                  _halo_spec(tm, hr, width, a_blk), _halo_spec(tm, hr, width, a_blk + 1),
                  pl.BlockSpec((CF_CONV_W, width), lambda i, j: (0, 0)), vec, vec, vec],
        out_specs=pl.BlockSpec((tm, width), lambda i, j: (i, 0)),
        out_shape=jax.ShapeDtypeStruct((t, width), BF16),
        scratch_shapes=[pltpu.VMEM((hr + tm, width), F32)],
        compiler_params=_cparams(("parallel", "arbitrary")),
        name="conformer_conv",
    )(proj, proj, proj, proj, w, b, ln_g, ln_b)


def _ffn_conv_kernel(g_ref, v_ref, gh_ref, vh_ref, wg_ref, wv_ref, o_ref, sg_ref, sv_ref, *, tm, halo_rows):
    first = pl.program_id(0) == 0
    _fill_conv_scratch(sg_ref, gh_ref[...].astype(F32), g_ref[...].astype(F32), halo_rows, first)
    _fill_conv_scratch(sv_ref, vh_ref[...].astype(F32), v_ref[...].astype(F32), halo_rows, first)
    gate = _causal_taps(sg_ref, wg_ref, halo_rows, tm, FFN_CONV_W)
    val = _causal_taps(sv_ref, wv_ref, halo_rows, tm, FFN_CONV_W)
    o_ref[...] = (gate * _sigmoid(gate) * val).astype(o_ref.dtype)


def _ffn_conv(up, w, *, t, d_ff):
    tm = _tile(t, 512)
    tc = _tile(d_ff, 512)
    hr = BF16_SUBLANES
    nb = d_ff // tc
    return pl.pallas_call(
        functools.partial(_ffn_conv_kernel, tm=tm, halo_rows=hr),
        grid=(t // tm, nb),
        in_specs=[pl.BlockSpec((tm, tc), lambda i, j: (i, j)),
                  pl.BlockSpec((tm, tc), lambda i, j: (i, nb + j)),
                  _halo_spec(tm, hr, tc, 0), _halo_spec(tm, hr, tc, nb),
                  pl.BlockSpec((FFN_CONV_W, tc), lambda i, j: (0, j)),
                  pl.BlockSpec((FFN_CONV_W, tc), lambda i, j: (0, nb + j))],
        out_specs=pl.BlockSpec((tm, tc), lambda i, j: (i, j)),
        out_shape=jax.ShapeDtypeStruct((t, d_ff), BF16),
        scratch_shapes=[pltpu.VMEM((hr + tm, tc), F32), pltpu.VMEM((hr + tm, tc), F32)],
        compiler_params=_cparams(("parallel", "parallel")),
        name="ffn_conv",
    )(up, up, up, up, w, w)


def _merge_kernel(ya_ref, yb_ref, yc_ref, yd_ref, w_ref, ga_ref, gb_ref, gc_ref, gd_ref, o_ref):
    ys = (ya_ref, yb_ref, yc_ref, yd_ref)
    gs = (ga_ref, gb_ref, gc_ref, gd_ref)
    acc = None
    for b in range(4):
        term = gs[b][...].astype(F32) * _dot(ys[b][...], w_ref[b])
        acc = term if acc is None else acc + term
    o_ref[...] = acc.astype(o_ref.dtype)


def _merge(ys, w_branch, gates, *, t, d):
    width = ys[0].shape[1]
    tm, tn = _tile(t, 1024), _tile(d, 512)
    nb = d // tn
    y_spec = pl.BlockSpec((tm, width), lambda i, j: (i, 0))
    gate_spec = lambda b: pl.BlockSpec((tm, tn), lambda i, j: (i, b * nb + j))
    return pl.pallas_call(
        _merge_kernel,
        grid=(t // tm, nb),
        in_specs=[y_spec, y_spec, y_spec, y_spec,
                  pl.BlockSpec((4, width, tn), lambda i, j: (0, 0, j)),
                  gate_spec(0), gate_spec(1), gate_spec(2), gate_spec(3)],
        out_specs=pl.BlockSpec((tm, tn), lambda i, j: (i, j)),
        out_shape=jax.ShapeDtypeStruct((t, d), BF16),
        compiler_params=_cparams(("parallel", "parallel")),
        name="gated_merge",
    )(*ys, w_branch, gates, gates, gates, gates)


def _alibi_slopes(nsa_heads, diff_heads):
    n = nsa_heads + diff_heads
    s = 2.0 ** (-8.0 * np.arange(1, n + 1) / n)
    stride = n // diff_heads
    diff_idx = np.arange(diff_heads) * stride + stride - 1
    nsa_idx = np.setdiff1d(np.arange(n), diff_idx)
    return jnp.asarray(s[nsa_idx], F32), jnp.asarray(s[diff_idx], F32)


def _shared_tokens(t, ncp, nselp):
    n_cmp = (t - CMP_BLOCK) // CMP_STRIDE + 1
    n_sel = t // SEL_BLOCK
    cs = np.arange(n_cmp) * CMP_STRIDE
    ss = np.arange(n_sel) * SEL_BLOCK
    sh = np.clip(np.minimum(cs[:, None] + CMP_BLOCK, ss[None, :] + SEL_BLOCK)
                 - np.maximum(cs[:, None], ss[None, :]), 0, None).astype(np.float32)
    out = np.zeros((ncp, nselp), np.float32)
    out[:n_cmp, :n_sel] = sh
    return jnp.asarray(out, BF16)


def kernel(x, attn_norm, w_in, cmp_pe, cmp_w1, cmp_w2, diff_lambda, diff_subln, sc_conv, cf_conv_w, cf_conv_b,
           cf_ln_g, cf_ln_b, w_branch, w_out, ffn_norm, w_up, ffn_conv, w_down, final_norm):
    bsz, t, d = x.shape
    assert bsz == 1
    depth = w_in.shape[0]
    bw = d // 4
    g = NSA_GROUPS
    nsa_heads = bw // HEAD_DIM
    rep = nsa_heads // g
    diff_heads = bw // (2 * HEAD_DIM)
    d_ff = w_down.shape[1]
    kv_cols = 3 * 2 * g * HEAD_DIM
    gate_cols = 3 * nsa_heads
    assert gate_cols <= LANES and t % CMP_STRIDE == 0 and t % SEL_BLOCK == 0
    ncp = t // CMP_STRIDE
    n_sel = t // SEL_BLOCK
    nselp = -(-n_sel // LANES) * LANES
    top_n = min(SEL_TOPN, n_sel)
    tq = _tile(t, 256)
    assert t >= tq + WINDOW

    slopes_nsa, slopes_diff = _alibi_slopes(nsa_heads, diff_heads)
    shared = _shared_tokens(t, ncp, nselp)

    c_qkv = bw + kv_cols
    c_gate = c_qkv + gate_cols
    c_mix = c_gate + 3 * bw + 3 * bw + 2 * bw
    kv_blk = bw // HEAD_DIM

    xs = x.reshape(t, d)
    for l in range(depth):
        wl = w_in[l]
        w_qkv = wl[:, :c_qkv].astype(BF16)
        w_gate = jnp.pad(wl[:, c_qkv:c_gate], ((0, 0), (0, LANES - gate_cols))).astype(BF16)
        w_mix = wl[:, c_gate:c_mix].astype(BF16)
        w_mgate = wl[:, c_mix:].astype(BF16)

        h = _rmsnorm(xs, attn_norm[l], BF16)
        qkv = _matmul(h, w_qkv, out_dtype=BF16, name="proj_qkv")
        nsa_gate = _matmul(h, w_gate, out_dtype=F32, name="proj_nsa_gate")
        mix = _matmul(h, w_mix, out_dtype=BF16, name="proj_mix")
        mgate = _matmul(h, w_mgate, out_dtype=BF16, act="sigmoid", name="proj_merge_gate")

        chunks = qkv[:, bw:bw + 2 * g * HEAD_DIM].reshape(ncp, CMP_STRIDE, 2, g, HEAD_DIM)
        chunks = chunks.transpose(2, 3, 0, 1, 4).reshape(2, g, ncp, CMP_STRIDE * HEAD_DIM)
        pe = jnp.broadcast_to(cmp_pe[l].reshape(2, 1, CMP_BLOCK * HEAD_DIM),
                              (2, BF16_SUBLANES, CMP_BLOCK * HEAD_DIM)).astype(BF16)
        kv_cmp = _compress(chunks, pe, cmp_w1[l].astype(BF16), cmp_w2[l].astype(BF16))
        o_cmp, imp = _cmp_attn(slopes_nsa, qkv, kv_cmp, shared, t=t, g=g, rep=rep, tq=tq)
        sel = _topk_mask(imp, top_n=top_n)
        o_sel = _sel_attn(slopes_nsa, qkv, sel, t=t, g=g, rep=rep, tq=tq,
                          k_blk=kv_blk + 2 * g, v_blk=kv_blk + 3 * g)
        y_a = _win_attn(slopes_nsa, qkv, nsa_gate, o_cmp, o_sel, t=t, g=g, rep=rep, tq=tq,
                        k_blk=kv_blk + 4 * g, v_blk=kv_blk + 5 * g)

        lam_init = 0.8 - 0.6 * math.exp(-0.3 * l)
        y_b = _diff_attn(slopes_diff, jnp.full((1,), lam_init, F32), mix, diff_lambda[l].astype(F32),
                         diff_subln[l].reshape(1, 2 * HEAD_DIM).astype(F32), t=t, heads=diff_heads, tq=tq,
                         q_blk=0, k_blk=diff_heads, v_blk=2 * diff_heads)

        tc = _tile(bw, 512)
        y_c = _sconv(mix, sc_conv[l].astype(F32), t=t, width=bw, b_blk=3 * bw // tc, tc=tc)
        row = lambda v: v.reshape(1, bw).astype(F32)
        y_d = _conformer(mix, cf_conv_w[l].astype(F32), row(cf_conv_b[l]), row(cf_ln_g[l]), row(cf_ln_b[l]),
                         t=t, width=bw, a_blk=6)

        merged = _merge((y_a, y_b, y_c, y_d), w_branch[l].astype(BF16), mgate, t=t, d=d)
        xs = _matmul(merged, w_out[l].astype(BF16), out_dtype=F32, res=xs, name="out_proj")

        h = _rmsnorm(xs, ffn_norm[l], BF16)
        up = _matmul(h, w_up[l].astype(BF16), out_dtype=BF16, name="ffn_up")
        act = _ffn_conv(up, ffn_conv[l].astype(F32), t=t, d_ff=d_ff)
        xs = _matmul(act, w_down[l].astype(BF16), out_dtype=F32, res=xs, tn=512, tk=d_ff // 2, name="ffn_down")

    return _rmsnorm(xs, final_norm, F32).reshape(bsz, t, d)
```

```python
import functools
import math

import numpy as np
import jax
import jax.numpy as jnp
from jax import lax
from jax.experimental import pallas as pl
from jax.experimental.pallas import tpu as pltpu

F32 = jnp.float32
BF16 = jnp.bfloat16

HEAD_DIM = 128
NSA_GROUPS = 2
CMP_BLOCK = 32
CMP_STRIDE = 16
SEL_BLOCK = 64
SEL_BLOCK_LOG2 = 6
SEL_TOPN = 16
WINDOW = 512
SC_CONV_W = 3
CF_CONV_W = 31
FFN_CONV_W = 3
NORM_EPS = 1e-6
NEG = -1e30
LOG2E = 1.4426950408889634
LANES = 128
BF16_SUBLANES = 16
VMEM_LIMIT_BYTES = 56 * 1024 * 1024


def _cparams(sem):
    return pltpu.CompilerParams(dimension_semantics=sem, vmem_limit_bytes=VMEM_LIMIT_BYTES)


def _tile(dim, pref):
    t = min(pref, dim)
    while dim % t:
        t //= 2
    return t


def _sigmoid(x):
    return 1.0 / (1.0 + jnp.exp(-x))


def _dot(a, b):
    return jnp.dot(a, b, preferred_element_type=F32)


def _dot_nt(a, b):
    return lax.dot_general(a, b, (((1,), (1,)), ((), ())), preferred_element_type=F32)


def _rmsnorm_kernel(x_ref, g_ref, o_ref):
    x = x_ref[...]
    y = x * lax.rsqrt(jnp.mean(x * x, axis=-1, keepdims=True) + NORM_EPS)
    o_ref[...] = (y * g_ref[...]).astype(o_ref.dtype)


def _rmsnorm(x, g, out_dtype):
    t, d = x.shape
    tm = _tile(t, 256)
    return pl.pallas_call(
        _rmsnorm_kernel,
        grid=(t // tm,),
        in_specs=[pl.BlockSpec((tm, d), lambda i: (i, 0)), pl.BlockSpec((1, d), lambda i: (0, 0))],
        out_specs=pl.BlockSpec((tm, d), lambda i: (i, 0)),
        out_shape=jax.ShapeDtypeStruct((t, d), out_dtype),
        compiler_params=_cparams(("parallel",)),
        name="rmsnorm",
    )(x, g.reshape(1, d).astype(F32))


def _mm_kernel(*refs, act, has_scale, has_res, nk):
    refs = list(refs)
    a_ref, w_ref = refs.pop(0), refs.pop(0)
    scale_ref = refs.pop(0) if has_scale else None
    res_ref = refs.pop(0) if has_res else None
    o_ref = refs.pop(0)
    part = _dot(a_ref[...], w_ref[...])

    def finish(acc):
        if has_scale:
            acc = acc * scale_ref[...]
        if act == "sigmoid":
            acc = _sigmoid(acc)
        if has_res:
            acc = acc + res_ref[...]
        o_ref[...] = acc.astype(o_ref.dtype)

    if nk == 1:
        finish(part)
    else:
        acc_ref = refs[-1]
        k = pl.program_id(2)

        @pl.when(k == 0)
        def _():
            acc_ref[...] = part

        @pl.when(k > 0)
        def _():
            acc_ref[...] += part

        @pl.when(k == nk - 1)
        def _():
            finish(acc_ref[...])


def _matmul(a, w, layer, *, out_dtype, n=None, col0=0, col_scale=None, act=None, res=None, tm=1024, tn=1024,
            tk=4096, name="matmul"):
    m, kd = a.shape
    n = w.shape[2] if n is None else n
    tm, tn, tk = _tile(m, tm), _tile(math.gcd(n, col0) if col0 else n, tn), _tile(kd, tk)
    nk = kd // tk
    j0 = col0 // tn
    in_specs = [pl.BlockSpec((tm, tk), lambda i, j, k: (i, k)),
                pl.BlockSpec((None, tk, tn), lambda i, j, k: (layer, k, j0 + j))]
    args = [a, w]
    if col_scale is not None:
        in_specs.append(pl.BlockSpec((1, tn), lambda i, j, k: (0, j)))
        args.append(col_scale)
    if res is not None:
        in_specs.append(pl.BlockSpec((tm, tn), lambda i, j, k: (i, j)))
        args.append(res)
    return pl.pallas_call(
        functools.partial(_mm_kernel, act=act, has_scale=col_scale is not None, has_res=res is not None, nk=nk),
        grid=(m // tm, n // tn, nk),
        in_specs=in_specs,
        out_specs=pl.BlockSpec((tm, tn), lambda i, j, k: (i, j)),
        out_shape=jax.ShapeDtypeStruct((m, n), out_dtype),
        scratch_shapes=[pltpu.VMEM((tm, tn), F32)] if nk > 1 else [],
        compiler_params=_cparams(("parallel", "parallel", "arbitrary")),
        name=name,
    )(*args)


def _gelu_tanh(x):
    return 0.5 * x * (1.0 + jnp.tanh(math.sqrt(2.0 / math.pi) * (x + 0.044715 * (x * x * x))))


def _compress_kernel(r_ref, pe_ref, w1_ref, w2_ref, o_ref, *, ncp, half):
    r = r_ref[...]
    u = _dot(r, w1_ref[:half, :])
    v = _dot(r, w1_ref[half:, :])
    c = _dot(pe_ref[...], w1_ref[...])[0:1]
    pre = u + pltpu.roll(v, ncp - 1, 0) + c
    out = _dot(_gelu_tanh(pre).astype(BF16), w2_ref[...])
    row = lax.broadcasted_iota(jnp.int32, out.shape, 0)
    o_ref[...] = jnp.where(row < ncp - 1, out, 0.0).astype(o_ref.dtype)


def _compress(chunks, pe, w1, w2):
    _, g, ncp, half = chunks.shape
    return pl.pallas_call(
        functools.partial(_compress_kernel, ncp=ncp, half=half),
        grid=(2, g),
        in_specs=[pl.BlockSpec((None, None, ncp, half), lambda a, b: (a, b, 0, 0)),
                  pl.BlockSpec((None, BF16_SUBLANES, 2 * half), lambda a, b: (a, 0, 0)),
                  pl.BlockSpec((None, 2 * half, HEAD_DIM), lambda a, b: (a, 0, 0)),
                  pl.BlockSpec((None, HEAD_DIM, HEAD_DIM), lambda a, b: (a, 0, 0))],
        out_specs=pl.BlockSpec((None, None, ncp, HEAD_DIM), lambda a, b: (a, b, 0, 0)),
        out_shape=jax.ShapeDtypeStruct((2, g, ncp, HEAD_DIM), BF16),
        compiler_params=_cparams(("parallel", "parallel")),
        name="nsa_compress",
    )(chunks, pe, w1, w2)


def _cmp_attn_kernel(slopes_ref, q_ref, k_ref, v_ref, shared_ref, o_ref, imp_ref, *, tq, rep):
    g, i = pl.program_id(0), pl.program_id(1)
    kc, vc = k_ref[...], v_ref[...]
    ncp = kc.shape[0]
    tpos = i * tq + lax.broadcasted_iota(jnp.int32, (tq, ncp), 0)
    cend = lax.broadcasted_iota(jnp.int32, (tq, ncp), 1) * CMP_STRIDE + (CMP_BLOCK - 1)
    dist = tpos - cend
    vis = dist >= 0
    distf = dist.astype(F32)
    anyvis = tpos >= CMP_BLOCK - 1
    psum = jnp.zeros((tq, ncp), F32)
    for r in range(rep):
        slope = slopes_ref[g * rep + r]
        s = _dot_nt(q_ref[:, r * HEAD_DIM:(r + 1) * HEAD_DIM], kc) - slope * distf
        s = jnp.where(vis, s, NEG)
        p = jnp.exp2(s - jnp.max(s, axis=-1, keepdims=True))
        p = jnp.where(anyvis, p / jnp.sum(p, axis=-1, keepdims=True), 0.0)
        o_ref[:, r * HEAD_DIM:(r + 1) * HEAD_DIM] = _dot(p.astype(BF16), vc)
        psum = psum + p
    hi = psum.astype(BF16)
    lo = (psum - hi.astype(F32)).astype(BF16)
    imp_ref[...] = _dot(hi, shared_ref[...]) + _dot(lo, shared_ref[...])


def _cmp_attn(slopes, qkv, kv_cmp, shared, *, t, g, rep, tq):
    ncp, nselp = shared.shape
    hw = rep * HEAD_DIM
    return pl.pallas_call(
        functools.partial(_cmp_attn_kernel, tq=tq, rep=rep),
        grid=(g, t // tq),
        in_specs=[pl.BlockSpec(memory_space=pltpu.SMEM),
                  pl.BlockSpec((tq, hw), lambda a, i: (i, a)),
                  pl.BlockSpec((None, None, ncp, HEAD_DIM), lambda a, i: (0, a, 0, 0)),
                  pl.BlockSpec((None, None, ncp, HEAD_DIM), lambda a, i: (1, a, 0, 0)),
                  pl.BlockSpec((ncp, nselp), lambda a, i: (0, 0))],
        out_specs=[pl.BlockSpec((tq, hw), lambda a, i: (i, a)),
                   pl.BlockSpec((None, tq, nselp), lambda a, i: (a, i, 0))],
        out_shape=[jax.ShapeDtypeStruct((t, g * hw), F32),
                   jax.ShapeDtypeStruct((g, t, nselp), F32)],
        compiler_params=_cparams(("parallel", "parallel")),
        name="nsa_cmp_attn",
    )(slopes, qkv, kv_cmp, kv_cmp, shared)


def _topk_kernel(imp_ref, sel_ref, *, tq, top_n):
    i = pl.program_id(1)
    imp = imp_ref[...].T
    blk = lax.broadcasted_iota(jnp.int32, imp.shape, 0)
    qblk = jnp.right_shift(i * tq + lax.broadcasted_iota(jnp.int32, imp.shape, 1), SEL_BLOCK_LOG2)
    forced = (blk == 0) | (blk == qblk) | (blk == qblk - 1)
    vals = jnp.where(forced, -NEG, imp)
    vals = jnp.where(blk <= qblk, vals, NEG)
    blkf = blk.astype(F32)
    nblk = float(imp.shape[0])

    def body(_, carry):
        vals, sel = carry
        m = jnp.max(vals, axis=0, keepdims=True)
        first = jnp.min(jnp.where(vals == m, blkf, nblk), axis=0, keepdims=True)
        pick = blkf == first
        sel = jnp.where(pick & (m > 0.5 * NEG), 1.0, sel)
        return jnp.where(pick, -jnp.inf, vals), sel

    _, sel = lax.fori_loop(0, top_n, body, (vals, jnp.zeros(imp.shape, F32)))
    sel_ref[...] = sel.T.astype(sel_ref.dtype)


def _topk_mask(imp, *, top_n):
    g, t, nselp = imp.shape
    tq = _tile(t, 256)
    return pl.pallas_call(
        functools.partial(_topk_kernel, tq=tq, top_n=top_n),
        grid=(g, t // tq),
        in_specs=[pl.BlockSpec((None, tq, nselp), lambda a, i: (a, i, 0))],
        out_specs=pl.BlockSpec((None, tq, nselp), lambda a, i: (a, i, 0)),
        out_shape=jax.ShapeDtypeStruct((g, t, nselp), BF16),
        compiler_params=_cparams(("parallel", "parallel")),
        name="nsa_topk",
    )(imp)


M_INIT = 0.5 * NEG


def _flash_init(m_ref, l_ref, acc_ref):
    m_ref[...] = jnp.full(m_ref.shape, M_INIT, F32)
    l_ref[...] = jnp.zeros(l_ref.shape, F32)
    acc_ref[...] = jnp.zeros(acc_ref.shape, F32)


def _flash_update(s, vt, m_ref, l_ref, acc_ref, idx):
    tk = s.shape[1]
    m_old = m_ref[idx]
    m_new = jnp.maximum(m_old, jnp.max(s, axis=-1, keepdims=True))
    alpha = jnp.exp2(m_old - m_new)
    lsum = alpha * l_ref[idx]
    ps = []
    for j in range(tk // LANES):
        pj = jnp.exp2(s[:, j * LANES:(j + 1) * LANES] - m_new)
        lsum = lsum + pj
        ps.append(pj.astype(BF16))
    l_ref[idx] = lsum
    m_ref[idx] = m_new
    pv = _dot(jnp.concatenate(ps, axis=-1), vt)
    for c in range(pv.shape[1] // LANES):
        cols = slice(c * LANES, (c + 1) * LANES)
        acc_ref[idx, :, cols] = alpha * acc_ref[idx, :, cols] + pv[:, cols]


def _flash_result(l_ref, acc_ref, idx):
    inv = 1.0 / jnp.sum(l_ref[idx], axis=-1, keepdims=True)
    return acc_ref[idx] * inv


def _sel_attn_kernel(flags_ref, slopes_ref, q_ref, k_ref, v_ref, sel_ref, o_ref, m_ref, l_ref, acc_ref,
                     *, tq, tk, nk, rep):
    g, i = pl.program_id(0), pl.program_id(1)
    _flash_init(m_ref, l_ref, acc_ref)
    sel = sel_ref[...]
    nselp = sel.shape[1]
    blk_row = lax.broadcasted_iota(jnp.int32, (nselp, tk), 0)
    blk_col = jnp.right_shift(lax.broadcasted_iota(jnp.int32, (nselp, tk), 1), SEL_BLOCK_LOG2)
    reach = lax.broadcasted_iota(jnp.int32, (tq, tk), 0) - lax.broadcasted_iota(jnp.int32, (tq, tk), 1)
    col1 = lax.broadcasted_iota(jnp.int32, (1, tk), 1)
    flag_base = (g * pl.num_programs(1) + i) * nk

    def step(k, carry):
        @pl.when(flags_ref[flag_base + k] > 0)
        def _():
            start = pl.multiple_of(k * tk, tk)
            kt = k_ref[pl.ds(start, tk), :]
            vt = v_ref[pl.ds(start, tk), :]
            expand = (blk_row == blk_col + k * (tk // SEL_BLOCK)).astype(BF16)
            off = i * tq - k * tk
            valid = (_dot(sel, expand) > 0.5) & (reach + off >= 0)
            rel = (col1 - off).astype(F32)
            for r in range(rep):
                slope = slopes_ref[g * rep + r]
                s = _dot_nt(q_ref[:, r * HEAD_DIM:(r + 1) * HEAD_DIM], kt) + slope * rel
                _flash_update(jnp.where(valid, s, NEG), vt, m_ref, l_ref, acc_ref, r)
        return carry

    lax.fori_loop(0, (i * tq + tq - 1) // tk + 1, step, 0)
    for r in range(rep):
        o_ref[:, r * HEAD_DIM:(r + 1) * HEAD_DIM] = _flash_result(l_ref, acc_ref, r)


def _sel_attn(flags, slopes, qkv, sel, *, t, g, rep, tq, tk, k_blk, v_blk):
    nselp = sel.shape[2]
    hw = rep * HEAD_DIM
    return pl.pallas_call(
        functools.partial(_sel_attn_kernel, tq=tq, tk=tk, nk=t // tk, rep=rep),
        grid=(g, t // tq),
        in_specs=[pl.BlockSpec(memory_space=pltpu.SMEM),
                  pl.BlockSpec(memory_space=pltpu.SMEM),
                  pl.BlockSpec((tq, hw), lambda a, i: (i, a)),
                  pl.BlockSpec((t, HEAD_DIM), lambda a, i: (0, k_blk + a)),
                  pl.BlockSpec((t, HEAD_DIM), lambda a, i: (0, v_blk + a)),
                  pl.BlockSpec((None, tq, nselp), lambda a, i: (a, i, 0))],
        out_specs=pl.BlockSpec((tq, hw), lambda a, i: (i, a)),
        out_shape=jax.ShapeDtypeStruct((t, g * hw), F32),
        scratch_shapes=[pltpu.VMEM((rep, tq, LANES), F32), pltpu.VMEM((rep, tq, LANES), F32),
                        pltpu.VMEM((rep, tq, HEAD_DIM), F32)],
        compiler_params=_cparams(("parallel", "parallel")),
        name="nsa_sel_attn",
    )(flags, slopes, qkv, qkv, qkv, sel)


def _win_attn_kernel(slopes_ref, q_ref, k_ref, v_ref, gate_ref, ocmp_ref, osel_ref, o_ref, *, tq, rep):
    g, i = pl.program_id(0), pl.program_id(1)
    span = tq + WINDOW
    start = pl.multiple_of(jnp.maximum(i * tq - WINDOW, 0), tq)
    kt = k_ref[pl.ds(start, span), :]
    vt = v_ref[pl.ds(start, span), :]
    tpos = i * tq + lax.broadcasted_iota(jnp.int32, (tq, span), 0)
    kpos = start + lax.broadcasted_iota(jnp.int32, (tq, span), 1)
    dist = tpos - kpos
    valid = (dist >= 0) & (dist < WINDOW)
    distf = dist.astype(F32)
    gates = _sigmoid(gate_ref[...])
    lane = lax.broadcasted_iota(jnp.int32, gates.shape, 1)

    def gate_col(c):
        return jnp.sum(jnp.where(lane == c, gates, 0.0), axis=-1, keepdims=True)

    for r in range(rep):
        head = g * rep + r
        s = _dot_nt(q_ref[:, r * HEAD_DIM:(r + 1) * HEAD_DIM], kt) - slopes_ref[head] * distf
        s = jnp.where(valid, s, NEG)
        p = jnp.exp2(s - jnp.max(s, axis=-1, keepdims=True))
        p = p / jnp.sum(p, axis=-1, keepdims=True)
        o_win = _dot(p.astype(BF16), vt)
        cols = slice(r * HEAD_DIM, (r + 1) * HEAD_DIM)
        o = (gate_col(3 * head) * ocmp_ref[:, cols] + gate_col(3 * head + 1) * osel_ref[:, cols]
             + gate_col(3 * head + 2) * o_win)
        o_ref[:, cols] = o.astype(o_ref.dtype)


def _win_attn(slopes, qkv, gate_logits, o_cmp, o_sel, *, t, g, rep, tq, k_blk, v_blk):
    hw = rep * HEAD_DIM
    return pl.pallas_call(
        functools.partial(_win_attn_kernel, tq=tq, rep=rep),
        grid=(g, t // tq),
        in_specs=[pl.BlockSpec(memory_space=pltpu.SMEM),
                  pl.BlockSpec((tq, hw), lambda a, i: (i, a)),
                  pl.BlockSpec((t, HEAD_DIM), lambda a, i: (0, k_blk + a)),
                  pl.BlockSpec((t, HEAD_DIM), lambda a, i: (0, v_blk + a)),
                  pl.BlockSpec((tq, LANES), lambda a, i: (i, 0)),
                  pl.BlockSpec((tq, hw), lambda a, i: (i, a)),
                  pl.BlockSpec((tq, hw), lambda a, i: (i, a))],
        out_specs=pl.BlockSpec((tq, hw), lambda a, i: (i, a)),
        out_shape=jax.ShapeDtypeStruct((t, g * hw), BF16),
        compiler_params=_cparams(("parallel", "parallel")),
        name="nsa_win_attn",
    )(slopes, qkv, qkv, qkv, gate_logits, o_cmp, o_sel)


def _diff_attn_kernel(slopes_ref, lam_init_ref, q_ref, k_ref, v_ref, lam_ref, g_ref, o_ref,
                      m_ref, l_ref, acc_ref, *, tq, tk):
    h, i = pl.program_id(0), pl.program_id(1)
    slope = slopes_ref[h]
    _flash_init(m_ref, l_ref, acc_ref)
    col1 = lax.broadcasted_iota(jnp.int32, (1, tk), 1)
    reach = lax.broadcasted_iota(jnp.int32, (tq, tk), 0) - lax.broadcasted_iota(jnp.int32, (tq, tk), 1)

    def step(k, masked):
        start = pl.multiple_of(k * tk, tk)
        kt = k_ref[pl.ds(start, tk), :]
        vt = v_ref[pl.ds(start, tk), :]
        off = i * tq - k * tk
        bias = slope * (col1 - off).astype(F32)
        for c in range(2):
            cols = slice(c * HEAD_DIM, (c + 1) * HEAD_DIM)
            s = _dot_nt(q_ref[:, cols], kt[:, cols]) + bias
            if masked:
                s = jnp.where(reach + off >= 0, s, NEG)
            _flash_update(s, vt, m_ref, l_ref, acc_ref, c)

    def full_step(k, carry):
        step(k, False)
        return carry

    n_full = (i * tq) // tk
    lax.fori_loop(0, n_full, full_step, 0)
    step(n_full, True)

    lv = lam_ref[...]
    lam_init = lam_init_ref[0]
    lam = (jnp.exp(jnp.sum(lv[0:1] * lv[1:2], axis=-1, keepdims=True))
           - jnp.exp(jnp.sum(lv[2:3] * lv[3:4], axis=-1, keepdims=True)) + lam_init)
    o = _flash_result(l_ref, acc_ref, 0) - lam * _flash_result(l_ref, acc_ref, 1)
    o = o * lax.rsqrt(jnp.mean(o * o, axis=-1, keepdims=True) + NORM_EPS) * g_ref[...]
    o_ref[...] = (o * (1.0 - lam_init)).astype(o_ref.dtype)


def _diff_attn(slopes, lam_init, proj, lam_vec, subln_g, *, t, heads, tq, tk, q_blk, k_blk, v_blk):
    assert tk % tq == 0
    hw = 2 * HEAD_DIM
    return pl.pallas_call(
        functools.partial(_diff_attn_kernel, tq=tq, tk=tk),
        grid=(heads, t // tq),
        in_specs=[pl.BlockSpec(memory_space=pltpu.SMEM),
                  pl.BlockSpec(memory_space=pltpu.SMEM),
                  pl.BlockSpec((tq, hw), lambda h, i: (i, q_blk + h)),
                  pl.BlockSpec((t, hw), lambda h, i: (0, k_blk + h)),
                  pl.BlockSpec((t, hw), lambda h, i: (0, v_blk + h)),
                  pl.BlockSpec((4, HEAD_DIM), lambda h, i: (0, 0)),
                  pl.BlockSpec((1, hw), lambda h, i: (0, 0))],
        out_specs=pl.BlockSpec((tq, hw), lambda h, i: (i, h)),
        out_shape=jax.ShapeDtypeStruct((t, heads * hw), BF16),
        scratch_shapes=[pltpu.VMEM((2, tq, LANES), F32), pltpu.VMEM((2, tq, LANES), F32),
                        pltpu.VMEM((2, tq, hw), F32)],
        compiler_params=_cparams(("parallel", "parallel")),
        name="diff_attn",
    )(slopes, lam_init, proj, proj, proj, lam_vec, subln_g)


def _fill_conv_scratch(scr_ref, halo, cur, halo_rows, first_tile):
    scr_ref[0:halo_rows, :] = jnp.where(first_tile, 0.0, halo)
    scr_ref[halo_rows:, :] = cur


def _causal_taps(scr_ref, w_ref, halo_rows, rows, taps):
    acc = None
    for k in range(taps):
        term = w_ref[k:k + 1, :] * scr_ref[pl.ds(halo_rows - taps + 1 + k, rows), :]
        acc = term if acc is None else acc + term
    return acc


def _halo_spec(rows, halo_rows, width, col_blk):
    per = rows // halo_rows
    return pl.BlockSpec((halo_rows, width), lambda i, j: (jnp.maximum(i * per - 1, 0), col_blk + j))


def _sconv_kernel(b_ref, c_ref, x_ref, ch_ref, xh_ref, w_ref, o_ref, scr_ref, *, tm, halo_rows):
    first = pl.program_id(0) == 0
    halo = ch_ref[...].astype(F32) * xh_ref[...].astype(F32)
    cur = c_ref[...].astype(F32) * x_ref[...].astype(F32)
    _fill_conv_scratch(scr_ref, halo, cur, halo_rows, first)
    y = b_ref[...].astype(F32) * _causal_taps(scr_ref, w_ref, halo_rows, tm, SC_CONV_W)
    o_ref[...] = y.astype(o_ref.dtype)


def _sconv(proj, w, *, t, width, b_blk, tc):
    tm = _tile(t, 512)
    hr = BF16_SUBLANES
    nb = width // tc
    cur = lambda off: pl.BlockSpec((tm, tc), lambda i, j: (i, off + j))
    return pl.pallas_call(
        functools.partial(_sconv_kernel, tm=tm, halo_rows=hr),
        grid=(t // tm, nb),
        in_specs=[cur(b_blk), cur(b_blk + nb), cur(b_blk + 2 * nb),
                  _halo_spec(tm, hr, tc, b_blk + nb), _halo_spec(tm, hr, tc, b_blk + 2 * nb),
                  pl.BlockSpec((SC_CONV_W, tc), lambda i, j: (0, j))],
        out_specs=pl.BlockSpec((tm, tc), lambda i, j: (i, j)),
        out_shape=jax.ShapeDtypeStruct((t, width), BF16),
        scratch_shapes=[pltpu.VMEM((hr + tm, tc), F32)],
        compiler_params=_cparams(("parallel", "parallel")),
        name="short_conv",
    )(proj, proj, proj, proj, proj, w)


def _conformer_kernel(a_ref, g_ref, ah_ref, gh_ref, w_ref, b_ref, lg_ref, lb_ref, o_ref, scr_ref, *, tm, halo_rows):
    first = pl.program_id(0) == 0
    halo = ah_ref[...].astype(F32) * _sigmoid(gh_ref[...].astype(F32))
    cur = a_ref[...].astype(F32) * _sigmoid(g_ref[...].astype(F32))
    _fill_conv_scratch(scr_ref, halo, cur, halo_rows, first)
    u = _causal_taps(scr_ref, w_ref, halo_rows, tm, CF_CONV_W) + b_ref[...]
    mu = jnp.mean(u, axis=-1, keepdims=True)
    uc = u - mu
    y = uc * lax.rsqrt(jnp.mean(uc * uc, axis=-1, keepdims=True) + NORM_EPS)
    y = y * lg_ref[...] + lb_ref[...]
    o_ref[...] = (y * _sigmoid(y)).astype(o_ref.dtype)


def _conformer(proj, w, b, ln_g, ln_b, *, t, width, a_blk):
    tm = _tile(t, 256)
    hr = 32
    vec = pl.BlockSpec((1, width), lambda i, j: (0, 0))
    return pl.pallas_call(
        functools.partial(_conformer_kernel, tm=tm, halo_rows=hr),
        grid=(t // tm, 1),
        in_specs=[pl.BlockSpec((tm, width), lambda i, j: (i, a_blk)),
                  pl.BlockSpec((tm, width), lambda i, j: (i, a_blk + 1)),
                  _halo_spec(tm, hr, width, a_blk), _halo_spec(tm, hr, width, a_blk + 1),
                  pl.BlockSpec((CF_CONV_W, width), lambda i, j: (0, 0)), vec, vec, vec],
        out_specs=pl.BlockSpec((tm, width), lambda i, j: (i, 0)),
        out_shape=jax.ShapeDtypeStruct((t, width), BF16),
        scratch_shapes=[pltpu.VMEM((hr + tm, width), F32)],
        compiler_params=_cparams(("parallel", "arbitrary")),
        name="conformer_conv",
    )(proj, proj, proj, proj, w, b, ln_g, ln_b)


def _ffn_conv_kernel(g_ref, v_ref, gh_ref, vh_ref, wg_ref, wv_ref, o_ref, sg_ref, sv_ref, *, tm, halo_rows):
    first = pl.program_id(0) == 0
    _fill_conv_scratch(sg_ref, gh_ref[...].astype(F32), g_ref[...].astype(F32), halo_rows, first)
    _fill_conv_scratch(sv_ref, vh_ref[...].astype(F32), v_ref[...].astype(F32), halo_rows, first)
    gate = _causal_taps(sg_ref, wg_ref, halo_rows, tm, FFN_CONV_W)
    val = _causal_taps(sv_ref, wv_ref, halo_rows, tm, FFN_CONV_W)
    o_ref[...] = (gate * _sigmoid(gate) * val).astype(o_ref.dtype)


def _ffn_conv(up, w, *, t, d_ff):
    tm = _tile(t, 512)
    tc = _tile(d_ff, 512)
    hr = BF16_SUBLANES
    nb = d_ff // tc
    return pl.pallas_call(
        functools.partial(_ffn_conv_kernel, tm=tm, halo_rows=hr),
        grid=(t // tm, nb),
        in_specs=[pl.BlockSpec((tm, tc), lambda i, j: (i, j)),
                  pl.BlockSpec((tm, tc), lambda i, j: (i, nb + j)),
                  _halo_spec(tm, hr, tc, 0), _halo_spec(tm, hr, tc, nb),
                  pl.BlockSpec((FFN_CONV_W, tc), lambda i, j: (0, j)),
                  pl.BlockSpec((FFN_CONV_W, tc), lambda i, j: (0, nb + j))],
        out_specs=pl.BlockSpec((tm, tc), lambda i, j: (i, j)),
        out_shape=jax.ShapeDtypeStruct((t, d_ff), BF16),
        scratch_shapes=[pltpu.VMEM((hr + tm, tc), F32), pltpu.VMEM((hr + tm, tc), F32)],
        compiler_params=_cparams(("parallel", "parallel")),
        name="ffn_conv",
    )(up, up, up, up, w, w)


def _merge_kernel(ya_ref, yb_ref, yc_ref, yd_ref, w_ref, ga_ref, gb_ref, gc_ref, gd_ref, o_ref):
    ys = (ya_ref, yb_ref, yc_ref, yd_ref)
    gs = (ga_ref, gb_ref, gc_ref, gd_ref)
    acc = None
    for b in range(4):
        term = gs[b][...].astype(F32) * _dot(ys[b][...], w_ref[b])
        acc = term if acc is None else acc + term
    o_ref[...] = acc.astype(o_ref.dtype)


def _merge(ys, w_branch, layer, gates, *, t, d):
    width = ys[0].shape[1]
    tm, tn = _tile(t, 1024), _tile(d, 512)
    nb = d // tn
    y_spec = pl.BlockSpec((tm, width), lambda i, j: (i, 0))
    gate_spec = lambda b: pl.BlockSpec((tm, tn), lambda i, j: (i, b * nb + j))
    return pl.pallas_call(
        _merge_kernel,
        grid=(t // tm, nb),
        in_specs=[y_spec, y_spec, y_spec, y_spec,
                  pl.BlockSpec((None, 4, width, tn), lambda i, j: (layer, 0, 0, j)),
                  gate_spec(0), gate_spec(1), gate_spec(2), gate_spec(3)],
        out_specs=pl.BlockSpec((tm, tn), lambda i, j: (i, j)),
        out_shape=jax.ShapeDtypeStruct((t, d), BF16),
        compiler_params=_cparams(("parallel", "parallel")),
        name="gated_merge",
    )(*ys, w_branch, gates, gates, gates, gates)


def _alibi_slopes(nsa_heads, diff_heads):
    n = nsa_heads + diff_heads
    s = 2.0 ** (-8.0 * np.arange(1, n + 1) / n)
    stride = n // diff_heads
    diff_idx = np.arange(diff_heads) * stride + stride - 1
    nsa_idx = np.setdiff1d(np.arange(n), diff_idx)
    return jnp.asarray(s[nsa_idx], F32), jnp.asarray(s[diff_idx], F32)


def _shared_tokens(t, ncp, nselp):
    n_cmp = (t - CMP_BLOCK) // CMP_STRIDE + 1
    n_sel = t // SEL_BLOCK
    cs = np.arange(n_cmp) * CMP_STRIDE
    ss = np.arange(n_sel) * SEL_BLOCK
    sh = np.clip(np.minimum(cs[:, None] + CMP_BLOCK, ss[None, :] + SEL_BLOCK)
                 - np.maximum(cs[:, None], ss[None, :]), 0, None).astype(np.float32)
    out = np.zeros((ncp, nselp), np.float32)
    out[:n_cmp, :n_sel] = sh
    return jnp.asarray(out, BF16)


def kernel(x, attn_norm, w_in, cmp_pe, cmp_w1, cmp_w2, diff_lambda, diff_subln, sc_conv, cf_conv_w, cf_conv_b,
           cf_ln_g, cf_ln_b, w_branch, w_out, ffn_norm, w_up, ffn_conv, w_down, final_norm):
    bsz, t, d = x.shape
    assert bsz == 1
    depth = w_in.shape[0]
    bw = d // 4
    g = NSA_GROUPS
    nsa_heads = bw // HEAD_DIM
    rep = nsa_heads // g
    diff_heads = bw // (2 * HEAD_DIM)
    d_ff = w_down.shape[1]
    kv_cols = 3 * 2 * g * HEAD_DIM
    gate_cols = 3 * nsa_heads
    assert gate_cols <= LANES and t % CMP_STRIDE == 0 and t % SEL_BLOCK == 0
    ncp = t // CMP_STRIDE
    n_sel = t // SEL_BLOCK
    nselp = -(-n_sel // LANES) * LANES
    top_n = min(SEL_TOPN, n_sel)
    tq = _tile(t, 256)
    assert t >= tq + WINDOW

    slopes_nsa, slopes_diff = _alibi_slopes(nsa_heads, diff_heads)
    slopes_nsa, slopes_diff = slopes_nsa * LOG2E, slopes_diff * LOG2E
    q_factor = HEAD_DIM ** -0.5 * LOG2E
    shared = _shared_tokens(t, ncp, nselp)

    c_qkv = bw + kv_cols
    c_gate = c_qkv + gate_cols
    c_mix = c_gate + 3 * bw + 3 * bw + 2 * bw
    kv_blk = bw // HEAD_DIM

    w_head = w_in[:, :, :c_qkv + LANES].astype(BF16)
    w_tail = w_in[:, :, c_gate:].astype(BF16)
    n_mix = c_mix - c_gate
    q_cols = lambda n: jnp.where(jnp.arange(n) < bw, q_factor, 1.0).astype(F32).reshape(1, n)
    w_branch_b, w_out_b = w_branch.astype(BF16), w_out.astype(BF16)
    w_up_b, w_down_b = w_up.astype(BF16), w_down.astype(BF16)
    cmp_w1_b, cmp_w2_b = cmp_w1.astype(BF16), cmp_w2.astype(BF16)
    tk_sel = _tile(t, 512)
    tk_diff = _tile(t, 512)
    tq_diff = _tile(t, 512)

    xs = x.reshape(t, d)
    for l in range(depth):
        h = _rmsnorm(xs, attn_norm[l], BF16)
        qkv = _matmul(h, w_head, l, n=c_qkv, col_scale=q_cols(c_qkv), out_dtype=BF16, name="proj_qkv")
        nsa_gate = _matmul(h, w_head, l, n=LANES, col0=c_qkv, out_dtype=F32, name="proj_nsa_gate")
        mix = _matmul(h, w_tail, l, n=n_mix, col_scale=q_cols(n_mix), out_dtype=BF16, name="proj_mix")
        mgate = _matmul(h, w_tail, l, n=4 * d, col0=n_mix, out_dtype=BF16, act="sigmoid", name="proj_merge_gate")

        chunks = qkv[:, bw:bw + 2 * g * HEAD_DIM].reshape(ncp, CMP_STRIDE, 2, g, HEAD_DIM)
        chunks = chunks.transpose(2, 3, 0, 1, 4).reshape(2, g, ncp, CMP_STRIDE * HEAD_DIM)
        pe = jnp.broadcast_to(cmp_pe[l].reshape(2, 1, CMP_BLOCK * HEAD_DIM),
                              (2, BF16_SUBLANES, CMP_BLOCK * HEAD_DIM)).astype(BF16)
        kv_cmp = _compress(chunks, pe, cmp_w1_b[l], cmp_w2_b[l])
        o_cmp, imp = _cmp_attn(slopes_nsa, qkv, kv_cmp, shared, t=t, g=g, rep=rep, tq=tq)
        sel = _topk_mask(imp, top_n=top_n)
        flags = sel[:, :, :n_sel].reshape(g, t // tq, tq, t // tk_sel, tk_sel // SEL_BLOCK).max(axis=(2, 4))
        flags = (flags > 0).astype(jnp.int32).reshape(-1)
        o_sel = _sel_attn(flags, slopes_nsa, qkv, sel, t=t, g=g, rep=rep, tq=tq, tk=tk_sel,
                          k_blk=kv_blk + 2 * g, v_blk=kv_blk + 3 * g)
        y_a = _win_attn(slopes_nsa, qkv, nsa_gate, o_cmp, o_sel, t=t, g=g, rep=rep, tq=tq,
                        k_blk=kv_blk + 4 * g, v_blk=kv_blk + 5 * g)

        lam_init = 0.8 - 0.6 * math.exp(-0.3 * l)
        y_b = _diff_attn(slopes_diff, jnp.full((1,), lam_init, F32), mix, diff_lambda[l].astype(F32),
                         diff_subln[l].reshape(1, 2 * HEAD_DIM).astype(F32), t=t, heads=diff_heads, tq=tq_diff,
                         tk=tk_diff, q_blk=0, k_blk=diff_heads, v_blk=2 * diff_heads)

        tc = _tile(bw, 512)
        y_c = _sconv(mix, sc_conv[l].astype(F32), t=t, width=bw, b_blk=3 * bw // tc, tc=tc)
        row = lambda v: v.reshape(1, bw).astype(F32)
        y_d = _conformer(mix, cf_conv_w[l].astype(F32), row(cf_conv_b[l]), row(cf_ln_g[l]), row(cf_ln_b[l]),
                         t=t, width=bw, a_blk=6)

        merged = _merge((y_a, y_b, y_c, y_d), w_branch_b, l, mgate, t=t, d=d)
        xs = _matmul(merged, w_out_b, l, out_dtype=F32, res=xs, name="out_proj")

        h = _rmsnorm(xs, ffn_norm[l], BF16)
        up = _matmul(h, w_up_b, l, out_dtype=BF16, name="ffn_up")
        act = _ffn_conv(up, ffn_conv[l].astype(F32), t=t, d_ff=d_ff)
        xs = _matmul(act, w_down_b, l, out_dtype=F32, res=xs, tk=d_ff // 2, name="ffn_down")

    return _rmsnorm(xs, final_norm, F32).reshape(bsz, t, d)
```

```python
import functools
import math

import numpy as np
import jax
import jax.numpy as jnp
from jax import lax
from jax.experimental import pallas as pl
from jax.experimental.pallas import tpu as pltpu

F32 = jnp.float32
BF16 = jnp.bfloat16

HEAD_DIM = 128
NSA_GROUPS = 2
CMP_BLOCK = 32
CMP_STRIDE = 16
SEL_BLOCK = 64
SEL_BLOCK_LOG2 = 6
SEL_TOPN = 16
WINDOW = 512
SC_CONV_W = 3
CF_CONV_W = 31
FFN_CONV_W = 3
NORM_EPS = 1e-6
NEG = -1e30
LOG2E = 1.4426950408889634
LANES = 128
F32_SUBLANES = 8
BF16_SUBLANES = 16
VMEM_LIMIT_BYTES = 56 * 1024 * 1024


def _cparams(sem):
    return pltpu.CompilerParams(dimension_semantics=sem, vmem_limit_bytes=VMEM_LIMIT_BYTES)


def _tile(dim, pref):
    t = min(pref, dim)
    while dim % t:
        t //= 2
    return t


def _sigmoid(x):
    return 1.0 / (1.0 + jnp.exp(-x))


def _dot(a, b):
    return jnp.dot(a, b, preferred_element_type=F32)


def _dot_nt(a, b):
    return lax.dot_general(a, b, (((1,), (1,)), ((), ())), preferred_element_type=F32)


def _rmsnorm_kernel(x_ref, g_ref, o_ref):
    x = x_ref[...]
    y = x * lax.rsqrt(jnp.mean(x * x, axis=-1, keepdims=True) + NORM_EPS)
    o_ref[...] = (y * g_ref[...]).astype(o_ref.dtype)


def _rmsnorm(x, g, out_dtype):
    t, d = x.shape
    tm = _tile(t, 256)
    return pl.pallas_call(
        _rmsnorm_kernel,
        grid=(t // tm,),
        in_specs=[pl.BlockSpec((tm, d), lambda i: (i, 0)), pl.BlockSpec((1, d), lambda i: (0, 0))],
        out_specs=pl.BlockSpec((tm, d), lambda i: (i, 0)),
        out_shape=jax.ShapeDtypeStruct((t, d), out_dtype),
        compiler_params=_cparams(("parallel",)),
        name="rmsnorm",
    )(x, g.reshape(1, d).astype(F32))


def _mm_kernel(*refs, act, has_scale, has_res, nk, w_is_nk):
    refs = list(refs)
    a_ref, w_ref = refs.pop(0), refs.pop(0)
    scale_ref = refs.pop(0) if has_scale else None
    res_ref = refs.pop(0) if has_res else None
    o_ref = refs.pop(0)
    part = (_dot_nt if w_is_nk else _dot)(a_ref[...], w_ref[...])

    def finish(acc):
        if has_scale:
            acc = acc * scale_ref[...]
        if act == "sigmoid":
            acc = _sigmoid(acc)
        if has_res:
            acc = acc + res_ref[...]
        o_ref[...] = acc.astype(o_ref.dtype)

    if nk == 1:
        finish(part)
    else:
        acc_ref = refs[-1]
        k = pl.program_id(2)

        @pl.when(k == 0)
        def _():
            acc_ref[...] = part

        @pl.when(k > 0)
        def _():
            acc_ref[...] += part

        @pl.when(k == nk - 1)
        def _():
            finish(acc_ref[...])


def _matmul(a, w, layer, *, out_dtype, n=None, col0=0, col_scale=None, act=None, res=None, w_is_nk=False, tm=1024,
            tn=1024, tk=4096, name="matmul"):
    m, kd = a.shape
    n = w.shape[1 if w_is_nk else 2] if n is None else n
    tm, tn, tk = _tile(m, tm), _tile(math.gcd(n, col0) if col0 else n, tn), _tile(kd, tk)
    nk = kd // tk
    j0 = col0 // tn
    in_specs = [pl.BlockSpec((tm, tk), lambda i, j, k: (i, k)),
                pl.BlockSpec((None, tn, tk), lambda i, j, k: (layer, j0 + j, k)) if w_is_nk else
                pl.BlockSpec((None, tk, tn), lambda i, j, k: (layer, k, j0 + j))]
    args = [a, w]
    if col_scale is not None:
        in_specs.append(pl.BlockSpec((1, tn), lambda i, j, k: (0, j)))
        args.append(col_scale)
    if res is not None:
        in_specs.append(pl.BlockSpec((tm, tn), lambda i, j, k: (i, j)))
        args.append(res)
    return pl.pallas_call(
        functools.partial(_mm_kernel, act=act, has_scale=col_scale is not None, has_res=res is not None, nk=nk,
                          w_is_nk=w_is_nk),
        grid=(m // tm, n // tn, nk),
        in_specs=in_specs,
        out_specs=pl.BlockSpec((tm, tn), lambda i, j, k: (i, j)),
        out_shape=jax.ShapeDtypeStruct((m, n), out_dtype),
        scratch_shapes=[pltpu.VMEM((tm, tn), F32)] if nk > 1 else [],
        compiler_params=_cparams(("parallel", "parallel", "arbitrary")),
        name=name,
    )(*args)


def _gelu_tanh(x):
    return 0.5 * x * (1.0 + jnp.tanh(math.sqrt(2.0 / math.pi) * (x + 0.044715 * (x * x * x))))


def _compress_kernel(r_ref, pe_ref, w1_ref, w2_ref, o_ref, *, ncp, half):
    r = r_ref[...]
    u = _dot(r, w1_ref[:half, :])
    v = _dot(r, w1_ref[half:, :])
    c = _dot(pe_ref[...], w1_ref[...])[0:1]
    pre = u + pltpu.roll(v, ncp - 1, 0) + c
    out = _dot(_gelu_tanh(pre).astype(BF16), w2_ref[...])
    row = lax.broadcasted_iota(jnp.int32, out.shape, 0)
    o_ref[...] = jnp.where(row < ncp - 1, out, 0.0).astype(o_ref.dtype)


def _compress(chunks, pe, w1, w2):
    _, g, ncp, half = chunks.shape
    return pl.pallas_call(
        functools.partial(_compress_kernel, ncp=ncp, half=half),
        grid=(2, g),
        in_specs=[pl.BlockSpec((None, None, ncp, half), lambda a, b: (a, b, 0, 0)),
                  pl.BlockSpec((None, BF16_SUBLANES, 2 * half), lambda a, b: (a, 0, 0)),
                  pl.BlockSpec((None, 2 * half, HEAD_DIM), lambda a, b: (a, 0, 0)),
                  pl.BlockSpec((None, HEAD_DIM, HEAD_DIM), lambda a, b: (a, 0, 0))],
        out_specs=pl.BlockSpec((None, None, ncp, HEAD_DIM), lambda a, b: (a, b, 0, 0)),
        out_shape=jax.ShapeDtypeStruct((2, g, ncp, HEAD_DIM), BF16),
        compiler_params=_cparams(("parallel", "parallel")),
        name="nsa_compress",
    )(chunks, pe, w1, w2)


def _cmp_attn_kernel(slopes_ref, q_ref, k_ref, v_ref, shared_ref, o_ref, imp_ref, *, tq, rep):
    g, i = pl.program_id(0), pl.program_id(1)
    kc, vc = k_ref[...], v_ref[...]
    ncp = kc.shape[0]
    tpos = i * tq + lax.broadcasted_iota(jnp.int32, (tq, ncp), 0)
    cend = lax.broadcasted_iota(jnp.int32, (tq, ncp), 1) * CMP_STRIDE + (CMP_BLOCK - 1)
    dist = tpos - cend
    vis = dist >= 0
    distf = dist.astype(F32)
    anyvis = tpos >= CMP_BLOCK - 1
    psum = jnp.zeros((tq, ncp), F32)
    for r in range(rep):
        slope = slopes_ref[g * rep + r]
        s = _dot_nt(q_ref[:, r * HEAD_DIM:(r + 1) * HEAD_DIM], kc) - slope * distf
        s = jnp.where(vis, s, NEG)
        p = jnp.exp2(s - jnp.max(s, axis=-1, keepdims=True))
        p = jnp.where(anyvis, p / jnp.sum(p, axis=-1, keepdims=True), 0.0)
        o_ref[:, r * HEAD_DIM:(r + 1) * HEAD_DIM] = _dot(p.astype(BF16), vc)
        psum = psum + p
    hi = psum.astype(BF16)
    lo = (psum - hi.astype(F32)).astype(BF16)
    imp_ref[...] = _dot(hi, shared_ref[...]) + _dot(lo, shared_ref[...])


def _cmp_attn(slopes, qkv, kv_cmp, shared, *, t, g, rep, tq):
    ncp, nselp = shared.shape
    hw = rep * HEAD_DIM
    return pl.pallas_call(
        functools.partial(_cmp_attn_kernel, tq=tq, rep=rep),
        grid=(g, t // tq),
        in_specs=[pl.BlockSpec(memory_space=pltpu.SMEM),
                  pl.BlockSpec((tq, hw), lambda a, i: (i, a)),
                  pl.BlockSpec((None, None, ncp, HEAD_DIM), lambda a, i: (0, a, 0, 0)),
                  pl.BlockSpec((None, None, ncp, HEAD_DIM), lambda a, i: (1, a, 0, 0)),
                  pl.BlockSpec((ncp, nselp), lambda a, i: (0, 0))],
        out_specs=[pl.BlockSpec((tq, hw), lambda a, i: (i, a)),
                   pl.BlockSpec((None, tq, nselp), lambda a, i: (a, i, 0))],
        out_shape=[jax.ShapeDtypeStruct((t, g * hw), F32),
                   jax.ShapeDtypeStruct((g, t, nselp), F32)],
        compiler_params=_cparams(("parallel", "parallel")),
        name="nsa_cmp_attn",
    )(slopes, qkv, kv_cmp, kv_cmp, shared)


def _topk_kernel(imp_ref, sel_ref, *, tq, top_n):
    i = pl.program_id(1)
    imp = imp_ref[...].T
    blk = lax.broadcasted_iota(jnp.int32, imp.shape, 0)
    qblk = jnp.right_shift(i * tq + lax.broadcasted_iota(jnp.int32, imp.shape, 1), SEL_BLOCK_LOG2)
    forced = (blk == 0) | (blk == qblk) | (blk == qblk - 1)
    vals = jnp.where(forced, -NEG, imp)
    vals = jnp.where(blk <= qblk, vals, NEG)
    blkf = blk.astype(F32)
    nblk = float(imp.shape[0])

    def body(_, carry):
        vals, sel = carry
        m = jnp.max(vals, axis=0, keepdims=True)
        first = jnp.min(jnp.where(vals == m, blkf, nblk), axis=0, keepdims=True)
        pick = blkf == first
        sel = jnp.where(pick & (m > 0.5 * NEG), 1.0, sel)
        return jnp.where(pick, -jnp.inf, vals), sel

    _, sel = lax.fori_loop(0, top_n, body, (vals, jnp.zeros(imp.shape, F32)))
    sel_ref[...] = sel.T.astype(sel_ref.dtype)


def _topk_mask(imp, *, top_n):
    g, t, nselp = imp.shape
    tq = _tile(t, 256)
    return pl.pallas_call(
        functools.partial(_topk_kernel, tq=tq, top_n=top_n),
        grid=(g, t // tq),
        in_specs=[pl.BlockSpec((None, tq, nselp), lambda a, i: (a, i, 0))],
        out_specs=pl.BlockSpec((None, tq, nselp), lambda a, i: (a, i, 0)),
        out_shape=jax.ShapeDtypeStruct((g, t, nselp), BF16),
        compiler_params=_cparams(("parallel", "parallel")),
        name="nsa_topk",
    )(imp)


M_INIT = 0.5 * NEG


def _flash_init(m_ref, l_ref, acc_ref):
    m_ref[...] = jnp.full(m_ref.shape, M_INIT, F32)
    l_ref[...] = jnp.zeros(l_ref.shape, F32)
    acc_ref[...] = jnp.zeros(acc_ref.shape, F32)


def _flash_update(s, vt, m_ref, l_ref, acc_ref, idx):
    tk = s.shape[1]
    m_old = m_ref[idx]
    m_new = jnp.maximum(m_old, jnp.max(s, axis=-1, keepdims=True))
    alpha = jnp.exp2(m_old - m_new)
    lsum = alpha * l_ref[idx]
    ps = []
    for j in range(tk // LANES):
        pj = jnp.exp2(s[:, j * LANES:(j + 1) * LANES] - m_new)
        lsum = lsum + pj
        ps.append(pj.astype(BF16))
    l_ref[idx] = lsum
    m_ref[idx] = m_new
    pv = _dot(jnp.concatenate(ps, axis=-1), vt)
    for c in range(pv.shape[1] // LANES):
        cols = slice(c * LANES, (c + 1) * LANES)
        acc_ref[idx, :, cols] = alpha * acc_ref[idx, :, cols] + pv[:, cols]


def _flash_result(l_ref, acc_ref, idx):
    inv = 1.0 / jnp.sum(l_ref[idx], axis=-1, keepdims=True)
    return acc_ref[idx] * inv


def _sel_attn_kernel(flags_ref, slopes_ref, q_ref, k_ref, v_ref, sel_ref, o_ref, m_ref, l_ref, acc_ref,
                     *, tq, tk, nk, rep):
    g, i = pl.program_id(0), pl.program_id(1)
    _flash_init(m_ref, l_ref, acc_ref)
    sel = sel_ref[...]
    nselp = sel.shape[1]
    blk_row = lax.broadcasted_iota(jnp.int32, (nselp, tk), 0)
    blk_col = jnp.right_shift(lax.broadcasted_iota(jnp.int32, (nselp, tk), 1), SEL_BLOCK_LOG2)
    reach = lax.broadcasted_iota(jnp.int32, (tq, tk), 0) - lax.broadcasted_iota(jnp.int32, (tq, tk), 1)
    col1 = lax.broadcasted_iota(jnp.int32, (1, tk), 1)
    flag_base = (g * pl.num_programs(1) + i) * nk

    def step(k, carry):
        @pl.when(flags_ref[flag_base + k] > 0)
        def _():
            start = pl.multiple_of(k * tk, tk)
            kt = k_ref[pl.ds(start, tk), :]
            vt = v_ref[pl.ds(start, tk), :]
            expand = (blk_row == blk_col + k * (tk // SEL_BLOCK)).astype(BF16)
            off = i * tq - k * tk
            valid = (_dot(sel, expand) > 0.5) & (reach + off >= 0)
            rel = (col1 - off).astype(F32)
            for r in range(rep):
                slope = slopes_ref[g * rep + r]
                s = _dot_nt(q_ref[:, r * HEAD_DIM:(r + 1) * HEAD_DIM], kt) + slope * rel
                _flash_update(jnp.where(valid, s, NEG), vt, m_ref, l_ref, acc_ref, r)
        return carry

    lax.fori_loop(0, (i * tq + tq - 1) // tk + 1, step, 0)
    for r in range(rep):
        o_ref[:, r * HEAD_DIM:(r + 1) * HEAD_DIM] = _flash_result(l_ref, acc_ref, r)


def _sel_attn(flags, slopes, qkv, sel, *, t, g, rep, tq, tk, k_blk, v_blk):
    nselp = sel.shape[2]
    hw = rep * HEAD_DIM
    return pl.pallas_call(
        functools.partial(_sel_attn_kernel, tq=tq, tk=tk, nk=t // tk, rep=rep),
        grid=(g, t // tq),
        in_specs=[pl.BlockSpec(memory_space=pltpu.SMEM),
                  pl.BlockSpec(memory_space=pltpu.SMEM),
                  pl.BlockSpec((tq, hw), lambda a, i: (i, a)),
                  pl.BlockSpec((t, HEAD_DIM), lambda a, i: (0, k_blk + a)),
                  pl.BlockSpec((t, HEAD_DIM), lambda a, i: (0, v_blk + a)),
                  pl.BlockSpec((None, tq, nselp), lambda a, i: (a, i, 0))],
        out_specs=pl.BlockSpec((tq, hw), lambda a, i: (i, a)),
        out_shape=jax.ShapeDtypeStruct((t, g * hw), F32),
        scratch_shapes=[pltpu.VMEM((rep, tq, LANES), F32), pltpu.VMEM((rep, tq, LANES), F32),
                        pltpu.VMEM((rep, tq, HEAD_DIM), F32)],
        compiler_params=_cparams(("parallel", "parallel")),
        name="nsa_sel_attn",
    )(flags, slopes, qkv, qkv, qkv, sel)


def _win_attn_kernel(slopes_ref, q_ref, k_ref, v_ref, gate_ref, ocmp_ref, osel_ref, o_ref, *, tq, rep):
    g, i = pl.program_id(0), pl.program_id(1)
    span = tq + WINDOW
    start = pl.multiple_of(jnp.maximum(i * tq - WINDOW, 0), tq)
    kt = k_ref[pl.ds(start, span), :]
    vt = v_ref[pl.ds(start, span), :]
    tpos = i * tq + lax.broadcasted_iota(jnp.int32, (tq, span), 0)
    kpos = start + lax.broadcasted_iota(jnp.int32, (tq, span), 1)
    dist = tpos - kpos
    valid = (dist >= 0) & (dist < WINDOW)
    distf = dist.astype(F32)
    gates = _sigmoid(gate_ref[...])
    lane = lax.broadcasted_iota(jnp.int32, gates.shape, 1)

    def gate_col(c):
        return jnp.sum(jnp.where(lane == c, gates, 0.0), axis=-1, keepdims=True)

    for r in range(rep):
        head = g * rep + r
        s = _dot_nt(q_ref[:, r * HEAD_DIM:(r + 1) * HEAD_DIM], kt) - slopes_ref[head] * distf
        s = jnp.where(valid, s, NEG)
        p = jnp.exp2(s - jnp.max(s, axis=-1, keepdims=True))
        p = p / jnp.sum(p, axis=-1, keepdims=True)
        o_win = _dot(p.astype(BF16), vt)
        cols = slice(r * HEAD_DIM, (r + 1) * HEAD_DIM)
        o = (gate_col(3 * head) * ocmp_ref[:, cols] + gate_col(3 * head + 1) * osel_ref[:, cols]
             + gate_col(3 * head + 2) * o_win)
        o_ref[:, cols] = o.astype(o_ref.dtype)


def _win_attn(slopes, qkv, gate_logits, o_cmp, o_sel, *, t, g, rep, tq, k_blk, v_blk):
    hw = rep * HEAD_DIM
    return pl.pallas_call(
        functools.partial(_win_attn_kernel, tq=tq, rep=rep),
        grid=(g, t // tq),
        in_specs=[pl.BlockSpec(memory_space=pltpu.SMEM),
                  pl.BlockSpec((tq, hw), lambda a, i: (i, a)),
                  pl.BlockSpec((t, HEAD_DIM), lambda a, i: (0, k_blk + a)),
                  pl.BlockSpec((t, HEAD_DIM), lambda a, i: (0, v_blk + a)),
                  pl.BlockSpec((tq, LANES), lambda a, i: (i, 0)),
                  pl.BlockSpec((tq, hw), lambda a, i: (i, a)),
                  pl.BlockSpec((tq, hw), lambda a, i: (i, a))],
        out_specs=pl.BlockSpec((tq, hw), lambda a, i: (i, a)),
        out_shape=jax.ShapeDtypeStruct((t, g * hw), BF16),
        compiler_params=_cparams(("parallel", "parallel")),
        name="nsa_win_attn",
    )(slopes, qkv, qkv, qkv, gate_logits, o_cmp, o_sel)


def _diff_attn_kernel(slopes_ref, lam_init_ref, q_ref, k_ref, v_ref, lam_ref, g_ref, o_ref,
                      m_ref, l_ref, acc_ref, *, tq, tk):
    h, i = pl.program_id(0), pl.program_id(1)
    slope = slopes_ref[h]
    _flash_init(m_ref, l_ref, acc_ref)
    col1 = lax.broadcasted_iota(jnp.int32, (1, tk), 1)
    reach = lax.broadcasted_iota(jnp.int32, (tq, tk), 0) - lax.broadcasted_iota(jnp.int32, (tq, tk), 1)

    def step(k, masked):
        start = pl.multiple_of(k * tk, tk)
        kt = k_ref[pl.ds(start, tk), :]
        vt = v_ref[pl.ds(start, tk), :]
        off = i * tq - k * tk
        bias = slope * (col1 - off).astype(F32)
        for c in range(2):
            cols = slice(c * HEAD_DIM, (c + 1) * HEAD_DIM)
            s = _dot_nt(q_ref[:, cols], kt[:, cols]) + bias
            if masked:
                s = jnp.where(reach + off >= 0, s, NEG)
            _flash_update(s, vt, m_ref, l_ref, acc_ref, c)

    def full_step(k, carry):
        step(k, False)
        return carry

    n_full = (i * tq) // tk
    lax.fori_loop(0, n_full, full_step, 0)
    step(n_full, True)

    lv = lam_ref[...]
    lam_init = lam_init_ref[0]
    lam = (jnp.exp(jnp.sum(lv[0:1] * lv[1:2], axis=-1, keepdims=True))
           - jnp.exp(jnp.sum(lv[2:3] * lv[3:4], axis=-1, keepdims=True)) + lam_init)
    o = _flash_result(l_ref, acc_ref, 0) - lam * _flash_result(l_ref, acc_ref, 1)
    o = o * lax.rsqrt(jnp.mean(o * o, axis=-1, keepdims=True) + NORM_EPS) * g_ref[...]
    o_ref[...] = (o * (1.0 - lam_init)).astype(o_ref.dtype)


def _diff_attn(slopes, lam_init, proj, lam_vec, subln_g, *, t, heads, tq, tk, q_blk, k_blk, v_blk):
    assert tk % tq == 0
    hw = 2 * HEAD_DIM
    return pl.pallas_call(
        functools.partial(_diff_attn_kernel, tq=tq, tk=tk),
        grid=(heads, t // tq),
        in_specs=[pl.BlockSpec(memory_space=pltpu.SMEM),
                  pl.BlockSpec(memory_space=pltpu.SMEM),
                  pl.BlockSpec((tq, hw), lambda h, i: (i, q_blk + h)),
                  pl.BlockSpec((t, hw), lambda h, i: (0, k_blk + h)),
                  pl.BlockSpec((t, hw), lambda h, i: (0, v_blk + h)),
                  pl.BlockSpec((4, HEAD_DIM), lambda h, i: (0, 0)),
                  pl.BlockSpec((1, hw), lambda h, i: (0, 0))],
        out_specs=pl.BlockSpec((tq, hw), lambda h, i: (i, h)),
        out_shape=jax.ShapeDtypeStruct((t, heads * hw), BF16),
        scratch_shapes=[pltpu.VMEM((2, tq, LANES), F32), pltpu.VMEM((2, tq, LANES), F32),
                        pltpu.VMEM((2, tq, hw), F32)],
        compiler_params=_cparams(("parallel", "parallel")),
        name="diff_attn",
    )(slopes, lam_init, proj, proj, proj, lam_vec, subln_g)


def _fill_conv_scratch(scr_ref, halo, cur, halo_rows, first_tile):
    scr_ref[0:halo_rows, :] = jnp.where(first_tile, 0.0, halo)
    scr_ref[halo_rows:, :] = cur


def _causal_taps(scr_ref, w_ref, halo_rows, rows, taps):
    acc = None
    for k in range(taps):
        term = w_ref[k:k + 1, :] * scr_ref[pl.ds(halo_rows - taps + 1 + k, rows), :]
        acc = term if acc is None else acc + term
    return acc


def _halo_spec(rows, halo_rows, width, col_blk):
    per = rows // halo_rows
    return pl.BlockSpec((halo_rows, width), lambda i, j: (jnp.maximum(i * per - 1, 0), col_blk + j))


def _sconv_kernel(b_ref, c_ref, x_ref, ch_ref, xh_ref, w_ref, o_ref, scr_ref, *, tm, halo_rows):
    first = pl.program_id(0) == 0
    halo = ch_ref[...].astype(F32) * xh_ref[...].astype(F32)
    cur = c_ref[...].astype(F32) * x_ref[...].astype(F32)
    _fill_conv_scratch(scr_ref, halo, cur, halo_rows, first)
    y = b_ref[...].astype(F32) * _causal_taps(scr_ref, w_ref, halo_rows, tm, SC_CONV_W)
    o_ref[...] = y.astype(o_ref.dtype)


def _sconv(proj, w, *, t, width, b_blk, tc):
    tm = _tile(t, 512)
    hr = BF16_SUBLANES
    nb = width // tc
    cur = lambda off: pl.BlockSpec((tm, tc), lambda i, j: (i, off + j))
    return pl.pallas_call(
        functools.partial(_sconv_kernel, tm=tm, halo_rows=hr),
        grid=(t // tm, nb),
        in_specs=[cur(b_blk), cur(b_blk + nb), cur(b_blk + 2 * nb),
                  _halo_spec(tm, hr, tc, b_blk + nb), _halo_spec(tm, hr, tc, b_blk + 2 * nb),
                  pl.BlockSpec((SC_CONV_W, tc), lambda i, j: (0, j))],
        out_specs=pl.BlockSpec((tm, tc), lambda i, j: (i, j)),
        out_shape=jax.ShapeDtypeStruct((t, width), BF16),
        scratch_shapes=[pltpu.VMEM((hr + tm, tc), F32)],
        compiler_params=_cparams(("parallel", "parallel")),
        name="short_conv",
    )(proj, proj, proj, proj, proj, w)


def _causal_taps_banked(scr_ref, bank_ref, w_ref, halo_rows, rows, taps):
    total = halo_rows + rows
    for b in range(1, F32_SUBLANES):
        bank_ref[b - 1, F32_SUBLANES:, :] = scr_ref[pl.ds(F32_SUBLANES - b, total - F32_SUBLANES), :]
    acc = None
    for k in range(taps):
        a, b = divmod(taps - 1 - k, F32_SUBLANES)
        src = scr_ref if b == 0 else bank_ref.at[b - 1]
        term = w_ref[k:k + 1, :] * src[pl.ds(halo_rows - F32_SUBLANES * a, rows), :]
        acc = term if acc is None else acc + term
    return acc


def _conformer_kernel(a_ref, g_ref, ah_ref, gh_ref, w_ref, b_ref, lg_ref, lb_ref, o_ref, scr_ref, bank_ref,
                      *, tm, halo_rows):
    first = pl.program_id(0) == 0
    halo = ah_ref[...].astype(F32) * _sigmoid(gh_ref[...].astype(F32))
    cur = a_ref[...].astype(F32) * _sigmoid(g_ref[...].astype(F32))
    _fill_conv_scratch(scr_ref, halo, cur, halo_rows, first)
    u = _causal_taps_banked(scr_ref, bank_ref, w_ref, halo_rows, tm, CF_CONV_W) + b_ref[...]
    mu = jnp.mean(u, axis=-1, keepdims=True)
    uc = u - mu
    y = uc * lax.rsqrt(jnp.mean(uc * uc, axis=-1, keepdims=True) + NORM_EPS)
    y = y * lg_ref[...] + lb_ref[...]
    o_ref[...] = (y * _sigmoid(y)).astype(o_ref.dtype)


def _conformer(proj, w, b, ln_g, ln_b, *, t, width, a_blk):
    tm = _tile(t, 256)
    hr = 32
    vec = pl.BlockSpec((1, width), lambda i, j: (0, 0))
    return pl.pallas_call(
        functools.partial(_conformer_kernel, tm=tm, halo_rows=hr),
        grid=(t // tm, 1),
        in_specs=[pl.BlockSpec((tm, width), lambda i, j: (i, a_blk)),
                  pl.BlockSpec((tm, width), lambda i, j: (i, a_blk + 1)),
                  _halo_spec(tm, hr, width, a_blk), _halo_spec(tm, hr, width, a_blk + 1),
                  pl.BlockSpec((CF_CONV_W, width), lambda i, j: (0, 0)), vec, vec, vec],
        out_specs=pl.BlockSpec((tm, width), lambda i, j: (i, 0)),
        out_shape=jax.ShapeDtypeStruct((t, width), BF16),
        scratch_shapes=[pltpu.VMEM((hr + tm, width), F32), pltpu.VMEM((F32_SUBLANES - 1, hr + tm, width), F32)],
        compiler_params=_cparams(("parallel", "arbitrary")),
        name="conformer_conv",
    )(proj, proj, proj, proj, w, b, ln_g, ln_b)


def _ffn_conv_kernel(g_ref, v_ref, gh_ref, vh_ref, wg_ref, wv_ref, o_ref, *, tm, halo_rows):
    tc = g_ref.shape[1]
    first = pl.program_id(0) == 0
    r = lax.broadcasted_iota(jnp.int32, (2 * tm, tm), 0)
    c = lax.broadcasted_iota(jnp.int32, (2 * tm, tm), 1)
    shift = (c == jnp.where(r >= tm, r - tm - 2, r - 1)).astype(BF16)
    sh = _dot(shift, jnp.concatenate([g_ref[...], v_ref[...]], axis=1))
    hrow = lax.broadcasted_iota(jnp.int32, (halo_rows, tc), 0)

    def conv(x_ref, h_ref, w_ref, x1, x2):
        y = w_ref[2:3, :] * x_ref[...].astype(F32) + w_ref[1:2, :] * x1 + w_ref[0:1, :] * x2
        h = jnp.where(first, 0.0, h_ref[...].astype(F32))
        prev1, prev2 = h[halo_rows - 1:halo_rows], h[halo_rows - 2:halo_rows - 1]
        patch = jnp.where(hrow == 0, w_ref[1:2, :] * prev1 + w_ref[0:1, :] * prev2,
                          jnp.where(hrow == 1, w_ref[0:1, :] * prev1, 0.0))
        return jnp.concatenate([y[:halo_rows] + patch, y[halo_rows:]], axis=0)

    gate = conv(g_ref, gh_ref, wg_ref, sh[:tm, :tc], sh[tm:, :tc])
    val = conv(v_ref, vh_ref, wv_ref, sh[:tm, tc:], sh[tm:, tc:])
    o_ref[...] = (gate * _sigmoid(gate) * val).astype(o_ref.dtype)


def _ffn_conv(up, w, *, t, d_ff):
    tm = _tile(t, 256)
    tc = _tile(d_ff, 1024)
    hr = BF16_SUBLANES
    nb = d_ff // tc
    return pl.pallas_call(
        functools.partial(_ffn_conv_kernel, tm=tm, halo_rows=hr),
        grid=(t // tm, nb),
        in_specs=[pl.BlockSpec((tm, tc), lambda i, j: (i, j)),
                  pl.BlockSpec((tm, tc), lambda i, j: (i, nb + j)),
                  _halo_spec(tm, hr, tc, 0), _halo_spec(tm, hr, tc, nb),
                  pl.BlockSpec((FFN_CONV_W, tc), lambda i, j: (0, j)),
                  pl.BlockSpec((FFN_CONV_W, tc), lambda i, j: (0, nb + j))],
        out_specs=pl.BlockSpec((tm, tc), lambda i, j: (i, j)),
        out_shape=jax.ShapeDtypeStruct((t, d_ff), BF16),
        compiler_params=_cparams(("parallel", "parallel")),
        name="ffn_conv",
    )(up, up, up, up, w, w)


def _merge_kernel(ya_ref, yb_ref, yc_ref, yd_ref, w_ref, ga_ref, gb_ref, gc_ref, gd_ref, o_ref):
    ys = (ya_ref, yb_ref, yc_ref, yd_ref)
    gs = (ga_ref, gb_ref, gc_ref, gd_ref)
    acc = None
    for b in range(4):
        term = gs[b][...].astype(F32) * _dot(ys[b][...], w_ref[b])
        acc = term if acc is None else acc + term
    o_ref[...] = acc.astype(o_ref.dtype)


def _merge(ys, w_branch, layer, gates, *, t, d):
    width = ys[0].shape[1]
    tm, tn = _tile(t, 1024), _tile(d, 512)
    nb = d // tn
    y_spec = pl.BlockSpec((tm, width), lambda i, j: (i, 0))
    gate_spec = lambda b: pl.BlockSpec((tm, tn), lambda i, j: (i, b * nb + j))
    return pl.pallas_call(
        _merge_kernel,
        grid=(t // tm, nb),
        in_specs=[y_spec, y_spec, y_spec, y_spec,
                  pl.BlockSpec((None, 4, width, tn), lambda i, j: (layer, 0, 0, j)),
                  gate_spec(0), gate_spec(1), gate_spec(2), gate_spec(3)],
        out_specs=pl.BlockSpec((tm, tn), lambda i, j: (i, j)),
        out_shape=jax.ShapeDtypeStruct((t, d), BF16),
        compiler_params=_cparams(("parallel", "parallel")),
        name="gated_merge",
    )(*ys, w_branch, gates, gates, gates, gates)


def _alibi_slopes(nsa_heads, diff_heads):
    n = nsa_heads + diff_heads
    s = 2.0 ** (-8.0 * np.arange(1, n + 1) / n)
    stride = n // diff_heads
    diff_idx = np.arange(diff_heads) * stride + stride - 1
    nsa_idx = np.setdiff1d(np.arange(n), diff_idx)
    return jnp.asarray(s[nsa_idx], F32), jnp.asarray(s[diff_idx], F32)


def _shared_tokens(t, ncp, nselp):
    n_cmp = (t - CMP_BLOCK) // CMP_STRIDE + 1
    n_sel = t // SEL_BLOCK
    cs = np.arange(n_cmp) * CMP_STRIDE
    ss = np.arange(n_sel) * SEL_BLOCK
    sh = np.clip(np.minimum(cs[:, None] + CMP_BLOCK, ss[None, :] + SEL_BLOCK)
                 - np.maximum(cs[:, None], ss[None, :]), 0, None).astype(np.float32)
    out = np.zeros((ncp, nselp), np.float32)
    out[:n_cmp, :n_sel] = sh
    return jnp.asarray(out, BF16)


def kernel(x, attn_norm, w_in, cmp_pe, cmp_w1, cmp_w2, diff_lambda, diff_subln, sc_conv, cf_conv_w, cf_conv_b,
           cf_ln_g, cf_ln_b, w_branch, w_out, ffn_norm, w_up, ffn_conv, w_down, final_norm):
    bsz, t, d = x.shape
    assert bsz == 1
    depth = w_in.shape[0]
    bw = d // 4
    g = NSA_GROUPS
    nsa_heads = bw // HEAD_DIM
    rep = nsa_heads // g
    diff_heads = bw // (2 * HEAD_DIM)
    d_ff = w_down.shape[1]
    kv_cols = 3 * 2 * g * HEAD_DIM
    gate_cols = 3 * nsa_heads
    assert gate_cols <= LANES and t % CMP_STRIDE == 0 and t % SEL_BLOCK == 0
    ncp = t // CMP_STRIDE
    n_sel = t // SEL_BLOCK
    nselp = -(-n_sel // LANES) * LANES
    top_n = min(SEL_TOPN, n_sel)
    tq = _tile(t, 256)
    assert t >= tq + WINDOW

    slopes_nsa, slopes_diff = _alibi_slopes(nsa_heads, diff_heads)
    slopes_nsa, slopes_diff = slopes_nsa * LOG2E, slopes_diff * LOG2E
    q_factor = HEAD_DIM ** -0.5 * LOG2E
    shared = _shared_tokens(t, ncp, nselp)

    c_qkv = bw + kv_cols
    c_gate = c_qkv + gate_cols
    c_mix = c_gate + 3 * bw + 3 * bw + 2 * bw
    kv_blk = bw // HEAD_DIM

    w_head = jnp.swapaxes(w_in, 1, 2).astype(BF16)
    w_tail = w_head[:, c_gate:, :]
    n_mix = c_mix - c_gate
    q_cols = lambda n: jnp.where(jnp.arange(n) < bw, q_factor, 1.0).astype(F32).reshape(1, n)
    w_branch_b, w_out_b = w_branch.astype(BF16), w_out.astype(BF16)
    w_up_b, w_down_b = w_up.astype(BF16), w_down.astype(BF16)
    cmp_w1_b, cmp_w2_b = cmp_w1.astype(BF16), cmp_w2.astype(BF16)
    tk_sel = _tile(t, 512)
    tk_diff = _tile(t, 512)
    tq_diff = _tile(t, 512)

    xs = x.reshape(t, d)
    for l in range(depth):
        h = _rmsnorm(xs, attn_norm[l], BF16)
        qkv = _matmul(h, w_head, l, n=c_qkv, col_scale=q_cols(c_qkv), out_dtype=BF16, w_is_nk=True, name="proj_qkv")
        nsa_gate = _matmul(h, w_head, l, n=LANES, col0=c_qkv, out_dtype=F32, w_is_nk=True, name="proj_nsa_gate")
        mix = _matmul(h, w_tail, l, n=n_mix, col_scale=q_cols(n_mix), out_dtype=BF16, w_is_nk=True, name="proj_mix")
        mgate = _matmul(h, w_tail, l, n=4 * d, col0=n_mix, out_dtype=BF16, act="sigmoid", w_is_nk=True, name="proj_merge_gate")

        chunks = qkv[:, bw:bw + 2 * g * HEAD_DIM].reshape(ncp, CMP_STRIDE, 2, g, HEAD_DIM)
        chunks = chunks.transpose(2, 3, 0, 1, 4).reshape(2, g, ncp, CMP_STRIDE * HEAD_DIM)
        pe = jnp.broadcast_to(cmp_pe[l].reshape(2, 1, CMP_BLOCK * HEAD_DIM),
                              (2, BF16_SUBLANES, CMP_BLOCK * HEAD_DIM)).astype(BF16)
        kv_cmp = _compress(chunks, pe, cmp_w1_b[l], cmp_w2_b[l])
        o_cmp, imp = _cmp_attn(slopes_nsa, qkv, kv_cmp, shared, t=t, g=g, rep=rep, tq=tq)
        sel = _topk_mask(imp, top_n=top_n)
        flags = sel[:, :, :n_sel].reshape(g, t // tq, tq, t // tk_sel, tk_sel // SEL_BLOCK).max(axis=(2, 4))
        flags = (flags > 0).astype(jnp.int32).reshape(-1)
        o_sel = _sel_attn(flags, slopes_nsa, qkv, sel, t=t, g=g, rep=rep, tq=tq, tk=tk_sel,
                          k_blk=kv_blk + 2 * g, v_blk=kv_blk + 3 * g)
        y_a = _win_attn(slopes_nsa, qkv, nsa_gate, o_cmp, o_sel, t=t, g=g, rep=rep, tq=tq,
                        k_blk=kv_blk + 4 * g, v_blk=kv_blk + 5 * g)

        lam_init = 0.8 - 0.6 * math.exp(-0.3 * l)
        y_b = _diff_attn(slopes_diff, jnp.full((1,), lam_init, F32), mix, diff_lambda[l].astype(F32),
                         diff_subln[l].reshape(1, 2 * HEAD_DIM).astype(F32), t=t, heads=diff_heads, tq=tq_diff,
                         tk=tk_diff, q_blk=0, k_blk=diff_heads, v_blk=2 * diff_heads)

        tc = _tile(bw, 512)
        y_c = _sconv(mix, sc_conv[l].astype(F32), t=t, width=bw, b_blk=3 * bw // tc, tc=tc)
        row = lambda v: v.reshape(1, bw).astype(F32)
        y_d = _conformer(mix, cf_conv_w[l].astype(F32), row(cf_conv_b[l]), row(cf_ln_g[l]), row(cf_ln_b[l]),
                         t=t, width=bw, a_blk=6)

        merged = _merge((y_a, y_b, y_c, y_d), w_branch_b, l, mgate, t=t, d=d)
        xs = _matmul(merged, w_out_b, l, out_dtype=F32, res=xs, name="out_proj")

        h = _rmsnorm(xs, ffn_norm[l], BF16)
        up = _matmul(h, w_up_b, l, out_dtype=BF16, name="ffn_up")
        act = _ffn_conv(up, ffn_conv[l].astype(F32), t=t, d_ff=d_ff)
        xs = _matmul(act, w_down_b, l, out_dtype=F32, res=xs, tk=d_ff // 2, name="ffn_down")

    return _rmsnorm(xs, final_norm, F32).reshape(bsz, t, d)
```

```python
import functools
import math

import numpy as np
import jax
import jax.numpy as jnp
from jax import lax
from jax.experimental import pallas as pl
from jax.experimental.pallas import tpu as pltpu

F32 = jnp.float32
BF16 = jnp.bfloat16

HEAD_DIM = 128
NSA_GROUPS = 2
CMP_BLOCK = 32
CMP_STRIDE = 16
SEL_BLOCK = 64
SEL_BLOCK_LOG2 = 6
SEL_TOPN = 16
WINDOW = 512
SC_CONV_W = 3
CF_CONV_W = 31
FFN_CONV_W = 3
NORM_EPS = 1e-6
NEG = -1e30
LOG2E = 1.4426950408889634
LANES = 128
F32_SUBLANES = 8
BF16_SUBLANES = 16
VMEM_LIMIT_BYTES = 60 * 1024 * 1024


def _cparams(sem):
    return pltpu.CompilerParams(dimension_semantics=sem, vmem_limit_bytes=VMEM_LIMIT_BYTES)


def _tile(dim, pref):
    t = min(pref, dim)
    while dim % t:
        t //= 2
    return t


def _sigmoid(x):
    return 1.0 / (1.0 + jnp.exp(-x))


def _dot(a, b):
    return jnp.dot(a, b, preferred_element_type=F32)


def _dot_nt(a, b):
    return lax.dot_general(a, b, (((1,), (1,)), ((), ())), preferred_element_type=F32)


def _rmsnorm_kernel(x_ref, g_ref, o_ref):
    x = x_ref[...]
    y = x * lax.rsqrt(jnp.mean(x * x, axis=-1, keepdims=True) + NORM_EPS)
    o_ref[...] = (y * g_ref[...]).astype(o_ref.dtype)


def _rmsnorm(x, g, out_dtype):
    t, d = x.shape
    tm = _tile(t, 256)
    return pl.pallas_call(
        _rmsnorm_kernel,
        grid=(t // tm,),
        in_specs=[pl.BlockSpec((tm, d), lambda i: (i, 0)), pl.BlockSpec((1, d), lambda i: (0, 0))],
        out_specs=pl.BlockSpec((tm, d), lambda i: (i, 0)),
        out_shape=jax.ShapeDtypeStruct((t, d), out_dtype),
        compiler_params=_cparams(("parallel",)),
        name="rmsnorm",
    )(x, g.reshape(1, d).astype(F32))


def _mm_kernel(*refs, act, has_scale, has_res, nk, w_is_nk):
    refs = list(refs)
    a_ref, w_ref = refs.pop(0), refs.pop(0)
    scale_ref = refs.pop(0) if has_scale else None
    res_ref = refs.pop(0) if has_res else None
    o_ref = refs.pop(0)
    if w_is_nk:
        part = _dot_nt(a_ref[...], w_ref[0].astype(BF16))
    else:
        part = _dot(a_ref[...], w_ref[...].astype(BF16))

    def finish(acc):
        if has_scale:
            acc = acc * scale_ref[...]
        if act == "sigmoid":
            acc = _sigmoid(acc)
        if has_res:
            acc = acc + res_ref[...]
        o_ref[...] = acc.astype(o_ref.dtype)

    if nk == 1:
        finish(part)
    else:
        acc_ref = refs[-1]
        k = pl.program_id(2)

        @pl.when(k == 0)
        def _():
            acc_ref[...] = part

        @pl.when(k > 0)
        def _():
            acc_ref[...] += part

        @pl.when(k == nk - 1)
        def _():
            finish(acc_ref[...])


def _matmul(a, w, layer, *, out_dtype, n=None, col0=0, col_scale=None, act=None, res=None, w_is_nk=False, tm=1024,
            tn=1024, tk=4096, name="matmul"):
    m, kd = a.shape
    n = w.shape[1 if w_is_nk else 2] if n is None else n
    tm, tk = _tile(m, tm), _tile(kd, tk)
    nk = kd // tk
    if w_is_nk:
        assert w.dtype == F32 and col0 % F32_SUBLANES == 0
        tn = _tile(n, tn)
        w_spec = pl.BlockSpec((pl.Element(1), pl.Element(tn), pl.Element(tk)),
                              lambda i, j, k: (layer, pl.multiple_of(col0 + j * tn, F32_SUBLANES),
                                               pl.multiple_of(k * tk, LANES)))
    else:
        tn = _tile(math.gcd(n, col0) if col0 else n, tn)
        j0 = col0 // tn
        w_spec = pl.BlockSpec((None, tk, tn), lambda i, j, k: (layer, k, j0 + j))
    in_specs = [pl.BlockSpec((tm, tk), lambda i, j, k: (i, k)), w_spec]
    args = [a, w]
    if col_scale is not None:
        in_specs.append(pl.BlockSpec((1, tn), lambda i, j, k: (0, j)))
        args.append(col_scale)
    if res is not None:
        in_specs.append(pl.BlockSpec((tm, tn), lambda i, j, k: (i, j)))
        args.append(res)
    return pl.pallas_call(
        functools.partial(_mm_kernel, act=act, has_scale=col_scale is not None, has_res=res is not None, nk=nk,
                          w_is_nk=w_is_nk),
        grid=(m // tm, n // tn, nk),
        in_specs=in_specs,
        out_specs=pl.BlockSpec((tm, tn), lambda i, j, k: (i, j)),
        out_shape=jax.ShapeDtypeStruct((m, n), out_dtype),
        scratch_shapes=[pltpu.VMEM((tm, tn), F32)] if nk > 1 else [],
        compiler_params=_cparams(("parallel", "parallel", "arbitrary")),
        name=name,
    )(*args)


def _gelu_tanh(x):
    return 0.5 * x * (1.0 + jnp.tanh(math.sqrt(2.0 / math.pi) * (x + 0.044715 * (x * x * x))))


def _compress_kernel(r_ref, pe_ref, w1_ref, w2_ref, o_ref, *, ncp, half):
    r = r_ref[...]
    u = _dot(r, w1_ref[:half, :])
    v = _dot(r, w1_ref[half:, :])
    c = _dot(pe_ref[...], w1_ref[...])[0:1]
    pre = u + pltpu.roll(v, ncp - 1, 0) + c
    out = _dot(_gelu_tanh(pre).astype(BF16), w2_ref[...])
    row = lax.broadcasted_iota(jnp.int32, out.shape, 0)
    o_ref[...] = jnp.where(row < ncp - 1, out, 0.0).astype(o_ref.dtype)


def _compress(chunks, pe, w1, w2):
    _, g, ncp, half = chunks.shape
    return pl.pallas_call(
        functools.partial(_compress_kernel, ncp=ncp, half=half),
        grid=(2, g),
        in_specs=[pl.BlockSpec((None, None, ncp, half), lambda a, b: (a, b, 0, 0)),
                  pl.BlockSpec((None, BF16_SUBLANES, 2 * half), lambda a, b: (a, 0, 0)),
                  pl.BlockSpec((None, 2 * half, HEAD_DIM), lambda a, b: (a, 0, 0)),
                  pl.BlockSpec((None, HEAD_DIM, HEAD_DIM), lambda a, b: (a, 0, 0))],
        out_specs=pl.BlockSpec((None, None, ncp, HEAD_DIM), lambda a, b: (a, b, 0, 0)),
        out_shape=jax.ShapeDtypeStruct((2, g, ncp, HEAD_DIM), BF16),
        compiler_params=_cparams(("parallel", "parallel")),
        name="nsa_compress",
    )(chunks, pe, w1, w2)


def _cmp_attn_kernel(slopes_ref, q_ref, k_ref, v_ref, shared_ref, o_ref, imp_ref, *, tq, rep):
    g, i = pl.program_id(0), pl.program_id(1)
    kc, vc = k_ref[...], v_ref[...]
    ncp = kc.shape[0]
    tpos = i * tq + lax.broadcasted_iota(jnp.int32, (tq, ncp), 0)
    cend = lax.broadcasted_iota(jnp.int32, (tq, ncp), 1) * CMP_STRIDE + (CMP_BLOCK - 1)
    dist = tpos - cend
    vis = dist >= 0
    distf = dist.astype(F32)
    anyvis = tpos >= CMP_BLOCK - 1
    psum = jnp.zeros((tq, ncp), F32)
    for r in range(rep):
        slope = slopes_ref[g * rep + r]
        s = _dot_nt(q_ref[:, r * HEAD_DIM:(r + 1) * HEAD_DIM], kc) - slope * distf
        s = jnp.where(vis, s, NEG)
        p = jnp.exp2(s - jnp.max(s, axis=-1, keepdims=True))
        p = jnp.where(anyvis, p / jnp.sum(p, axis=-1, keepdims=True), 0.0)
        o_ref[:, r * HEAD_DIM:(r + 1) * HEAD_DIM] = _dot(p.astype(BF16), vc)
        psum = psum + p
    hi = psum.astype(BF16)
    lo = (psum - hi.astype(F32)).astype(BF16)
    imp_ref[...] = _dot(hi, shared_ref[...]) + _dot(lo, shared_ref[...])


def _cmp_attn(slopes, qkv, kv_cmp, shared, *, t, g, rep, tq):
    ncp, nselp = shared.shape
    hw = rep * HEAD_DIM
    return pl.pallas_call(
        functools.partial(_cmp_attn_kernel, tq=tq, rep=rep),
        grid=(g, t // tq),
        in_specs=[pl.BlockSpec(memory_space=pltpu.SMEM),
                  pl.BlockSpec((tq, hw), lambda a, i: (i, a)),
                  pl.BlockSpec((None, None, ncp, HEAD_DIM), lambda a, i: (0, a, 0, 0)),
                  pl.BlockSpec((None, None, ncp, HEAD_DIM), lambda a, i: (1, a, 0, 0)),
                  pl.BlockSpec((ncp, nselp), lambda a, i: (0, 0))],
        out_specs=[pl.BlockSpec((tq, hw), lambda a, i: (i, a)),
                   pl.BlockSpec((None, tq, nselp), lambda a, i: (a, i, 0))],
        out_shape=[jax.ShapeDtypeStruct((t, g * hw), F32),
                   jax.ShapeDtypeStruct((g, t, nselp), F32)],
        compiler_params=_cparams(("parallel", "parallel")),
        name="nsa_cmp_attn",
    )(slopes, qkv, kv_cmp, kv_cmp, shared)


def _topk_kernel(imp_ref, sel_ref, *, tq, top_n):
    i = pl.program_id(1)
    imp = imp_ref[...].T
    blk = lax.broadcasted_iota(jnp.int32, imp.shape, 0)
    qblk = jnp.right_shift(i * tq + lax.broadcasted_iota(jnp.int32, imp.shape, 1), SEL_BLOCK_LOG2)
    forced = (blk == 0) | (blk == qblk) | (blk == qblk - 1)
    vals = jnp.where(forced, -NEG, imp)
    vals = jnp.where(blk <= qblk, vals, NEG)
    blkf = blk.astype(F32)
    nblk = float(imp.shape[0])

    def body(_, carry):
        vals, sel = carry
        m = jnp.max(vals, axis=0, keepdims=True)
        first = jnp.min(jnp.where(vals == m, blkf, nblk), axis=0, keepdims=True)
        pick = blkf == first
        sel = jnp.where(pick & (m > 0.5 * NEG), 1.0, sel)
        return jnp.where(pick, -jnp.inf, vals), sel

    _, sel = lax.fori_loop(0, top_n, body, (vals, jnp.zeros(imp.shape, F32)))
    sel_ref[...] = sel.T.astype(sel_ref.dtype)


def _topk_mask(imp, *, top_n):
    g, t, nselp = imp.shape
    tq = _tile(t, 256)
    return pl.pallas_call(
        functools.partial(_topk_kernel, tq=tq, top_n=top_n),
        grid=(g, t // tq),
        in_specs=[pl.BlockSpec((None, tq, nselp), lambda a, i: (a, i, 0))],
        out_specs=pl.BlockSpec((None, tq, nselp), lambda a, i: (a, i, 0)),
        out_shape=jax.ShapeDtypeStruct((g, t, nselp), BF16),
        compiler_params=_cparams(("parallel", "parallel")),
        name="nsa_topk",
    )(imp)


M_INIT = 0.5 * NEG


def _flash_init(m_ref, l_ref, acc_ref):
    m_ref[...] = jnp.full(m_ref.shape, M_INIT, F32)
    l_ref[...] = jnp.zeros(l_ref.shape, F32)
    acc_ref[...] = jnp.zeros(acc_ref.shape, F32)


def _flash_update(s, vt, m_ref, l_ref, acc_ref, idx):
    tk = s.shape[1]
    m_old = m_ref[idx]
    m_new = jnp.maximum(m_old, jnp.max(s, axis=-1, keepdims=True))
    alpha = jnp.exp2(m_old - m_new)
    lsum = alpha * l_ref[idx]
    ps = []
    for j in range(tk // LANES):
        pj = jnp.exp2(s[:, j * LANES:(j + 1) * LANES] - m_new)
        lsum = lsum + pj
        ps.append(pj.astype(BF16))
    l_ref[idx] = lsum
    m_ref[idx] = m_new
    pv = _dot(jnp.concatenate(ps, axis=-1), vt)
    for c in range(pv.shape[1] // LANES):
        cols = slice(c * LANES, (c + 1) * LANES)
        acc_ref[idx, :, cols] = alpha * acc_ref[idx, :, cols] + pv[:, cols]


def _flash_result(l_ref, acc_ref, idx):
    inv = 1.0 / jnp.sum(l_ref[idx], axis=-1, keepdims=True)
    return acc_ref[idx] * inv


def _sel_attn_kernel(flags_ref, slopes_ref, q_ref, k_ref, v_ref, sel_ref, o_ref, m_ref, l_ref, acc_ref,
                     *, tq, tk, nk, rep):
    g, i = pl.program_id(0), pl.program_id(1)
    _flash_init(m_ref, l_ref, acc_ref)
    sel = sel_ref[...]
    nselp = sel.shape[1]
    blk_row = lax.broadcasted_iota(jnp.int32, (nselp, tk), 0)
    blk_col = jnp.right_shift(lax.broadcasted_iota(jnp.int32, (nselp, tk), 1), SEL_BLOCK_LOG2)
    reach = lax.broadcasted_iota(jnp.int32, (tq, tk), 0) - lax.broadcasted_iota(jnp.int32, (tq, tk), 1)
    col1 = lax.broadcasted_iota(jnp.int32, (1, tk), 1)
    flag_base = (g * pl.num_programs(1) + i) * nk

    def step(k, carry):
        @pl.when(flags_ref[flag_base + k] > 0)
        def _():
            start = pl.multiple_of(k * tk, tk)
            kt = k_ref[pl.ds(start, tk), :]
            vt = v_ref[pl.ds(start, tk), :]
            expand = (blk_row == blk_col + k * (tk // SEL_BLOCK)).astype(BF16)
            off = i * tq - k * tk
            valid = (_dot(sel, expand) > 0.5) & (reach + off >= 0)
            rel = (col1 - off).astype(F32)
            for r in range(rep):
                slope = slopes_ref[g * rep + r]
                s = _dot_nt(q_ref[:, r * HEAD_DIM:(r + 1) * HEAD_DIM], kt) + slope * rel
                _flash_update(jnp.where(valid, s, NEG), vt, m_ref, l_ref, acc_ref, r)
        return carry

    lax.fori_loop(0, (i * tq + tq - 1) // tk + 1, step, 0)
    for r in range(rep):
        o_ref[:, r * HEAD_DIM:(r + 1) * HEAD_DIM] = _flash_result(l_ref, acc_ref, r)


def _sel_attn(flags, slopes, qkv, sel, *, t, g, rep, tq, tk, k_blk, v_blk):
    nselp = sel.shape[2]
    hw = rep * HEAD_DIM
    return pl.pallas_call(
        functools.partial(_sel_attn_kernel, tq=tq, tk=tk, nk=t // tk, rep=rep),
        grid=(g, t // tq),
        in_specs=[pl.BlockSpec(memory_space=pltpu.SMEM),
                  pl.BlockSpec(memory_space=pltpu.SMEM),
                  pl.BlockSpec((tq, hw), lambda a, i: (i, a)),
                  pl.BlockSpec((t, HEAD_DIM), lambda a, i: (0, k_blk + a)),
                  pl.BlockSpec((t, HEAD_DIM), lambda a, i: (0, v_blk + a)),
                  pl.BlockSpec((None, tq, nselp), lambda a, i: (a, i, 0))],
        out_specs=pl.BlockSpec((tq, hw), lambda a, i: (i, a)),
        out_shape=jax.ShapeDtypeStruct((t, g * hw), F32),
        scratch_shapes=[pltpu.VMEM((rep, tq, LANES), F32), pltpu.VMEM((rep, tq, LANES), F32),
                        pltpu.VMEM((rep, tq, HEAD_DIM), F32)],
        compiler_params=_cparams(("parallel", "parallel")),
        name="nsa_sel_attn",
    )(flags, slopes, qkv, qkv, qkv, sel)


def _win_attn_kernel(slopes_ref, q_ref, k_ref, v_ref, gate_ref, ocmp_ref, osel_ref, o_ref, *, tq, rep):
    g, i = pl.program_id(0), pl.program_id(1)
    span = tq + WINDOW
    start = pl.multiple_of(jnp.maximum(i * tq - WINDOW, 0), tq)
    kt = k_ref[pl.ds(start, span), :]
    vt = v_ref[pl.ds(start, span), :]
    tpos = i * tq + lax.broadcasted_iota(jnp.int32, (tq, span), 0)
    kpos = start + lax.broadcasted_iota(jnp.int32, (tq, span), 1)
    dist = tpos - kpos
    valid = (dist >= 0) & (dist < WINDOW)
    distf = dist.astype(F32)
    gates = _sigmoid(gate_ref[...])
    lane = lax.broadcasted_iota(jnp.int32, gates.shape, 1)

    def gate_col(c):
        return jnp.sum(jnp.where(lane == c, gates, 0.0), axis=-1, keepdims=True)

    for r in range(rep):
        head = g * rep + r
        s = _dot_nt(q_ref[:, r * HEAD_DIM:(r + 1) * HEAD_DIM], kt) - slopes_ref[head] * distf
        s = jnp.where(valid, s, NEG)
        p = jnp.exp2(s - jnp.max(s, axis=-1, keepdims=True))
        p = p / jnp.sum(p, axis=-1, keepdims=True)
        o_win = _dot(p.astype(BF16), vt)
        cols = slice(r * HEAD_DIM, (r + 1) * HEAD_DIM)
        o = (gate_col(3 * head) * ocmp_ref[:, cols] + gate_col(3 * head + 1) * osel_ref[:, cols]
             + gate_col(3 * head + 2) * o_win)
        o_ref[:, cols] = o.astype(o_ref.dtype)


def _win_attn(slopes, qkv, gate_logits, o_cmp, o_sel, *, t, g, rep, tq, k_blk, v_blk):
    hw = rep * HEAD_DIM
    return pl.pallas_call(
        functools.partial(_win_attn_kernel, tq=tq, rep=rep),
        grid=(g, t // tq),
        in_specs=[pl.BlockSpec(memory_space=pltpu.SMEM),
                  pl.BlockSpec((tq, hw), lambda a, i: (i, a)),
                  pl.BlockSpec((t, HEAD_DIM), lambda a, i: (0, k_blk + a)),
                  pl.BlockSpec((t, HEAD_DIM), lambda a, i: (0, v_blk + a)),
                  pl.BlockSpec((tq, LANES), lambda a, i: (i, 0)),
                  pl.BlockSpec((tq, hw), lambda a, i: (i, a)),
                  pl.BlockSpec((tq, hw), lambda a, i: (i, a))],
        out_specs=pl.BlockSpec((tq, hw), lambda a, i: (i, a)),
        out_shape=jax.ShapeDtypeStruct((t, g * hw), BF16),
        compiler_params=_cparams(("parallel", "parallel")),
        name="nsa_win_attn",
    )(slopes, qkv, qkv, qkv, gate_logits, o_cmp, o_sel)


def _diff_attn_kernel(slopes_ref, lam_init_ref, q_ref, k_ref, v_ref, lam_ref, g_ref, o_ref,
                      m_ref, l_ref, acc_ref, *, tq, tk):
    h, i = pl.program_id(0), pl.program_id(1)
    slope = slopes_ref[h]
    _flash_init(m_ref, l_ref, acc_ref)
    col1 = lax.broadcasted_iota(jnp.int32, (1, tk), 1)
    reach = lax.broadcasted_iota(jnp.int32, (tq, tk), 0) - lax.broadcasted_iota(jnp.int32, (tq, tk), 1)

    def step(k, masked):
        start = pl.multiple_of(k * tk, tk)
        kt = k_ref[pl.ds(start, tk), :]
        vt = v_ref[pl.ds(start, tk), :]
        off = i * tq - k * tk
        bias = slope * (col1 - off).astype(F32)
        for c in range(2):
            cols = slice(c * HEAD_DIM, (c + 1) * HEAD_DIM)
            s = _dot_nt(q_ref[:, cols], kt[:, cols]) + bias
            if masked:
                s = jnp.where(reach + off >= 0, s, NEG)
            _flash_update(s, vt, m_ref, l_ref, acc_ref, c)

    def full_step(k, carry):
        step(k, False)
        return carry

    n_full = (i * tq) // tk
    lax.fori_loop(0, n_full, full_step, 0)
    step(n_full, True)

    lv = lam_ref[...]
    lam_init = lam_init_ref[0]
    lam = (jnp.exp(jnp.sum(lv[0:1] * lv[1:2], axis=-1, keepdims=True))
           - jnp.exp(jnp.sum(lv[2:3] * lv[3:4], axis=-1, keepdims=True)) + lam_init)
    o = _flash_result(l_ref, acc_ref, 0) - lam * _flash_result(l_ref, acc_ref, 1)
    o = o * lax.rsqrt(jnp.mean(o * o, axis=-1, keepdims=True) + NORM_EPS) * g_ref[...]
    o_ref[...] = (o * (1.0 - lam_init)).astype(o_ref.dtype)


def _diff_attn(slopes, lam_init, proj, lam_vec, subln_g, *, t, heads, tq, tk, q_blk, k_blk, v_blk):
    assert tk % tq == 0
    hw = 2 * HEAD_DIM
    return pl.pallas_call(
        functools.partial(_diff_attn_kernel, tq=tq, tk=tk),
        grid=(heads, t // tq),
        in_specs=[pl.BlockSpec(memory_space=pltpu.SMEM),
                  pl.BlockSpec(memory_space=pltpu.SMEM),
                  pl.BlockSpec((tq, hw), lambda h, i: (i, q_blk + h)),
                  pl.BlockSpec((t, hw), lambda h, i: (0, k_blk + h)),
                  pl.BlockSpec((t, hw), lambda h, i: (0, v_blk + h)),
                  pl.BlockSpec((4, HEAD_DIM), lambda h, i: (0, 0)),
                  pl.BlockSpec((1, hw), lambda h, i: (0, 0))],
        out_specs=pl.BlockSpec((tq, hw), lambda h, i: (i, h)),
        out_shape=jax.ShapeDtypeStruct((t, heads * hw), BF16),
        scratch_shapes=[pltpu.VMEM((2, tq, LANES), F32), pltpu.VMEM((2, tq, LANES), F32),
                        pltpu.VMEM((2, tq, hw), F32)],
        compiler_params=_cparams(("parallel", "parallel")),
        name="diff_attn",
    )(slopes, lam_init, proj, proj, proj, lam_vec, subln_g)


def _fill_conv_scratch(scr_ref, halo, cur, halo_rows, first_tile):
    scr_ref[0:halo_rows, :] = jnp.where(first_tile, 0.0, halo)
    scr_ref[halo_rows:, :] = cur


def _causal_taps(scr_ref, w_ref, halo_rows, rows, taps):
    acc = None
    for k in range(taps):
        term = w_ref[k:k + 1, :] * scr_ref[pl.ds(halo_rows - taps + 1 + k, rows), :]
        acc = term if acc is None else acc + term
    return acc


def _halo_spec(rows, halo_rows, width, col_blk):
    per = rows // halo_rows
    return pl.BlockSpec((halo_rows, width), lambda i, j: (jnp.maximum(i * per - 1, 0), col_blk + j))


def _sconv_kernel(b_ref, c_ref, x_ref, ch_ref, xh_ref, w_ref, o_ref, scr_ref, *, tm, halo_rows):
    first = pl.program_id(0) == 0
    halo = ch_ref[...].astype(F32) * xh_ref[...].astype(F32)
    cur = c_ref[...].astype(F32) * x_ref[...].astype(F32)
    _fill_conv_scratch(scr_ref, halo, cur, halo_rows, first)
    y = b_ref[...].astype(F32) * _causal_taps(scr_ref, w_ref, halo_rows, tm, SC_CONV_W)
    o_ref[...] = y.astype(o_ref.dtype)


def _sconv(proj, w, *, t, width, b_blk, tc):
    tm = _tile(t, 512)
    hr = BF16_SUBLANES
    nb = width // tc
    cur = lambda off: pl.BlockSpec((tm, tc), lambda i, j: (i, off + j))
    return pl.pallas_call(
        functools.partial(_sconv_kernel, tm=tm, halo_rows=hr),
        grid=(t // tm, nb),
        in_specs=[cur(b_blk), cur(b_blk + nb), cur(b_blk + 2 * nb),
                  _halo_spec(tm, hr, tc, b_blk + nb), _halo_spec(tm, hr, tc, b_blk + 2 * nb),
                  pl.BlockSpec((SC_CONV_W, tc), lambda i, j: (0, j))],
        out_specs=pl.BlockSpec((tm, tc), lambda i, j: (i, j)),
        out_shape=jax.ShapeDtypeStruct((t, width), BF16),
        scratch_shapes=[pltpu.VMEM((hr + tm, tc), F32)],
        compiler_params=_cparams(("parallel", "parallel")),
        name="short_conv",
    )(proj, proj, proj, proj, proj, w)


def _causal_taps_banked(scr_ref, bank_ref, w_ref, halo_rows, rows, taps):
    total = halo_rows + rows
    for b in range(1, F32_SUBLANES):
        bank_ref[b - 1, F32_SUBLANES:, :] = scr_ref[pl.ds(F32_SUBLANES - b, total - F32_SUBLANES), :]
    acc = None
    for k in range(taps):
        a, b = divmod(taps - 1 - k, F32_SUBLANES)
        src = scr_ref if b == 0 else bank_ref.at[b - 1]
        term = w_ref[k:k + 1, :] * src[pl.ds(halo_rows - F32_SUBLANES * a, rows), :]
        acc = term if acc is None else acc + term
    return acc


def _conformer_kernel(a_ref, g_ref, ah_ref, gh_ref, w_ref, b_ref, lg_ref, lb_ref, o_ref, scr_ref, bank_ref,
                      *, tm, halo_rows):
    first = pl.program_id(0) == 0
    halo = ah_ref[...].astype(F32) * _sigmoid(gh_ref[...].astype(F32))
    cur = a_ref[...].astype(F32) * _sigmoid(g_ref[...].astype(F32))
    _fill_conv_scratch(scr_ref, halo, cur, halo_rows, first)
    u = _causal_taps_banked(scr_ref, bank_ref, w_ref, halo_rows, tm, CF_CONV_W) + b_ref[...]
    mu = jnp.mean(u, axis=-1, keepdims=True)
    uc = u - mu
    y = uc * lax.rsqrt(jnp.mean(uc * uc, axis=-1, keepdims=True) + NORM_EPS)
    y = y * lg_ref[...] + lb_ref[...]
    o_ref[...] = (y * _sigmoid(y)).astype(o_ref.dtype)


def _conformer(proj, w, b, ln_g, ln_b, *, t, width, a_blk):
    tm = _tile(t, 256)
    hr = 32
    vec = pl.BlockSpec((1, width), lambda i, j: (0, 0))
    return pl.pallas_call(
        functools.partial(_conformer_kernel, tm=tm, halo_rows=hr),
        grid=(t // tm, 1),
        in_specs=[pl.BlockSpec((tm, width), lambda i, j: (i, a_blk)),
                  pl.BlockSpec((tm, width), lambda i, j: (i, a_blk + 1)),
                  _halo_spec(tm, hr, width, a_blk), _halo_spec(tm, hr, width, a_blk + 1),
                  pl.BlockSpec((CF_CONV_W, width), lambda i, j: (0, 0)), vec, vec, vec],
        out_specs=pl.BlockSpec((tm, width), lambda i, j: (i, 0)),
        out_shape=jax.ShapeDtypeStruct((t, width), BF16),
        scratch_shapes=[pltpu.VMEM((hr + tm, width), F32), pltpu.VMEM((F32_SUBLANES - 1, hr + tm, width), F32)],
        compiler_params=_cparams(("parallel", "arbitrary")),
        name="conformer_conv",
    )(proj, proj, proj, proj, w, b, ln_g, ln_b)


def _ffn_conv_kernel(g_ref, v_ref, gh_ref, vh_ref, wg_ref, wv_ref, o_ref, *, tm, halo_rows):
    tc = g_ref.shape[1]
    first = pl.program_id(0) == 0
    r = lax.broadcasted_iota(jnp.int32, (2 * tm, tm), 0)
    c = lax.broadcasted_iota(jnp.int32, (2 * tm, tm), 1)
    shift = (c == jnp.where(r >= tm, r - tm - 2, r - 1)).astype(BF16)
    sh = _dot(shift, jnp.concatenate([g_ref[...], v_ref[...]], axis=1))
    hrow = lax.broadcasted_iota(jnp.int32, (halo_rows, tc), 0)

    def conv(x_ref, h_ref, w_ref, x1, x2):
        y = w_ref[2:3, :] * x_ref[...].astype(F32) + w_ref[1:2, :] * x1 + w_ref[0:1, :] * x2
        h = jnp.where(first, 0.0, h_ref[...].astype(F32))
        prev1, prev2 = h[halo_rows - 1:halo_rows], h[halo_rows - 2:halo_rows - 1]
        patch = jnp.where(hrow == 0, w_ref[1:2, :] * prev1 + w_ref[0:1, :] * prev2,
                          jnp.where(hrow == 1, w_ref[0:1, :] * prev1, 0.0))
        return jnp.concatenate([y[:halo_rows] + patch, y[halo_rows:]], axis=0)

    gate = conv(g_ref, gh_ref, wg_ref, sh[:tm, :tc], sh[tm:, :tc])
    val = conv(v_ref, vh_ref, wv_ref, sh[:tm, tc:], sh[tm:, tc:])
    o_ref[...] = (gate * _sigmoid(gate) * val).astype(o_ref.dtype)


def _ffn_conv(up, w, *, t, d_ff):
    tm = _tile(t, 256)
    tc = _tile(d_ff, 1024)
    hr = BF16_SUBLANES
    nb = d_ff // tc
    return pl.pallas_call(
        functools.partial(_ffn_conv_kernel, tm=tm, halo_rows=hr),
        grid=(t // tm, nb),
        in_specs=[pl.BlockSpec((tm, tc), lambda i, j: (i, j)),
                  pl.BlockSpec((tm, tc), lambda i, j: (i, nb + j)),
                  _halo_spec(tm, hr, tc, 0), _halo_spec(tm, hr, tc, nb),
                  pl.BlockSpec((FFN_CONV_W, tc), lambda i, j: (0, j)),
                  pl.BlockSpec((FFN_CONV_W, tc), lambda i, j: (0, nb + j))],
        out_specs=pl.BlockSpec((tm, tc), lambda i, j: (i, j)),
        out_shape=jax.ShapeDtypeStruct((t, d_ff), BF16),
        compiler_params=_cparams(("parallel", "parallel")),
        name="ffn_conv",
    )(up, up, up, up, w, w)


def _merge_kernel(ya_ref, yb_ref, yc_ref, yd_ref, w_ref, ga_ref, gb_ref, gc_ref, gd_ref, o_ref):
    ys = (ya_ref, yb_ref, yc_ref, yd_ref)
    gs = (ga_ref, gb_ref, gc_ref, gd_ref)
    acc = None
    for b in range(4):
        term = gs[b][...].astype(F32) * _dot(ys[b][...], w_ref[b].astype(BF16))
        acc = term if acc is None else acc + term
    o_ref[...] = acc.astype(o_ref.dtype)


def _merge(ys, w_branch, layer, gates, *, t, d):
    width = ys[0].shape[1]
    tm, tn = _tile(t, 1024), _tile(d, 512)
    nb = d // tn
    y_spec = pl.BlockSpec((tm, width), lambda i, j: (i, 0))
    gate_spec = lambda b: pl.BlockSpec((tm, tn), lambda i, j: (i, b * nb + j))
    return pl.pallas_call(
        _merge_kernel,
        grid=(t // tm, nb),
        in_specs=[y_spec, y_spec, y_spec, y_spec,
                  pl.BlockSpec((None, 4, width, tn), lambda i, j: (layer, 0, 0, j)),
                  gate_spec(0), gate_spec(1), gate_spec(2), gate_spec(3)],
        out_specs=pl.BlockSpec((tm, tn), lambda i, j: (i, j)),
        out_shape=jax.ShapeDtypeStruct((t, d), BF16),
        compiler_params=_cparams(("parallel", "parallel")),
        name="gated_merge",
    )(*ys, w_branch, gates, gates, gates, gates)


def _alibi_slopes(nsa_heads, diff_heads):
    n = nsa_heads + diff_heads
    s = 2.0 ** (-8.0 * np.arange(1, n + 1) / n)
    stride = n // diff_heads
    diff_idx = np.arange(diff_heads) * stride + stride - 1
    nsa_idx = np.setdiff1d(np.arange(n), diff_idx)
    return jnp.asarray(s[nsa_idx], F32), jnp.asarray(s[diff_idx], F32)


def _shared_tokens(t, ncp, nselp):
    n_cmp = (t - CMP_BLOCK) // CMP_STRIDE + 1
    n_sel = t // SEL_BLOCK
    cs = np.arange(n_cmp) * CMP_STRIDE
    ss = np.arange(n_sel) * SEL_BLOCK
    sh = np.clip(np.minimum(cs[:, None] + CMP_BLOCK, ss[None, :] + SEL_BLOCK)
                 - np.maximum(cs[:, None], ss[None, :]), 0, None).astype(np.float32)
    out = np.zeros((ncp, nselp), np.float32)
    out[:n_cmp, :n_sel] = sh
    return jnp.asarray(out, BF16)


def kernel(x, attn_norm, w_in, cmp_pe, cmp_w1, cmp_w2, diff_lambda, diff_subln, sc_conv, cf_conv_w, cf_conv_b,
           cf_ln_g, cf_ln_b, w_branch, w_out, ffn_norm, w_up, ffn_conv, w_down, final_norm):
    bsz, t, d = x.shape
    assert bsz == 1
    depth = w_in.shape[0]
    bw = d // 4
    g = NSA_GROUPS
    nsa_heads = bw // HEAD_DIM
    rep = nsa_heads // g
    diff_heads = bw // (2 * HEAD_DIM)
    d_ff = w_down.shape[1]
    kv_cols = 3 * 2 * g * HEAD_DIM
    gate_cols = 3 * nsa_heads
    assert gate_cols <= LANES and t % CMP_STRIDE == 0 and t % SEL_BLOCK == 0
    ncp = t // CMP_STRIDE
    n_sel = t // SEL_BLOCK
    nselp = -(-n_sel // LANES) * LANES
    top_n = min(SEL_TOPN, n_sel)
    tq = _tile(t, 256)
    assert t >= tq + WINDOW

    slopes_nsa, slopes_diff = _alibi_slopes(nsa_heads, diff_heads)
    slopes_nsa, slopes_diff = slopes_nsa * LOG2E, slopes_diff * LOG2E
    q_factor = HEAD_DIM ** -0.5 * LOG2E
    shared = _shared_tokens(t, ncp, nselp)

    c_qkv = bw + kv_cols
    c_gate = c_qkv + gate_cols
    c_mix = c_gate + 3 * bw + 3 * bw + 2 * bw
    kv_blk = bw // HEAD_DIM

    w_in_t = jnp.swapaxes(w_in, 1, 2)
    w_tail, tail0 = (w_in_t, c_gate) if c_gate % F32_SUBLANES == 0 else (w_in_t[:, c_gate:, :], 0)
    n_mix = c_mix - c_gate
    q_cols = lambda n: jnp.where(jnp.arange(n) < bw, q_factor, 1.0).astype(F32).reshape(1, n)
    w_out_b, w_down_b = w_out.astype(BF16), w_down.astype(BF16)
    wide = dict(tm=2048, tn=512)
    cmp_w1_b, cmp_w2_b = cmp_w1.astype(BF16), cmp_w2.astype(BF16)
    tk_sel = _tile(t, 512)
    tk_diff = _tile(t, 512)
    tq_diff = _tile(t, 512)

    xs = x.reshape(t, d)
    for l in range(depth):
        h = _rmsnorm(xs, attn_norm[l], BF16)
        qkv = _matmul(h, w_in_t, l, n=c_qkv, col_scale=q_cols(c_qkv), out_dtype=BF16, w_is_nk=True, **wide,
                      name="proj_qkv")
        nsa_gate = _matmul(h, w_in_t, l, n=LANES, col0=c_qkv, out_dtype=F32, w_is_nk=True, **wide,
                           name="proj_nsa_gate")
        mix = _matmul(h, w_tail, l, n=n_mix, col0=tail0, col_scale=q_cols(n_mix), out_dtype=BF16, w_is_nk=True,
                      **wide, name="proj_mix")
        mgate = _matmul(h, w_tail, l, n=4 * d, col0=tail0 + n_mix, out_dtype=BF16, act="sigmoid", w_is_nk=True,
                        **wide, name="proj_merge_gate")

        chunks = qkv[:, bw:bw + 2 * g * HEAD_DIM].reshape(ncp, CMP_STRIDE, 2, g, HEAD_DIM)
        chunks = chunks.transpose(2, 3, 0, 1, 4).reshape(2, g, ncp, CMP_STRIDE * HEAD_DIM)
        pe = jnp.broadcast_to(cmp_pe[l].reshape(2, 1, CMP_BLOCK * HEAD_DIM),
                              (2, BF16_SUBLANES, CMP_BLOCK * HEAD_DIM)).astype(BF16)
        kv_cmp = _compress(chunks, pe, cmp_w1_b[l], cmp_w2_b[l])
        o_cmp, imp = _cmp_attn(slopes_nsa, qkv, kv_cmp, shared, t=t, g=g, rep=rep, tq=tq)
        sel = _topk_mask(imp, top_n=top_n)
        flags = sel[:, :, :n_sel].reshape(g, t // tq, tq, t // tk_sel, tk_sel // SEL_BLOCK).max(axis=(2, 4))
        flags = (flags > 0).astype(jnp.int32).reshape(-1)
        o_sel = _sel_attn(flags, slopes_nsa, qkv, sel, t=t, g=g, rep=rep, tq=tq, tk=tk_sel,
                          k_blk=kv_blk + 2 * g, v_blk=kv_blk + 3 * g)
        y_a = _win_attn(slopes_nsa, qkv, nsa_gate, o_cmp, o_sel, t=t, g=g, rep=rep, tq=tq,
                        k_blk=kv_blk + 4 * g, v_blk=kv_blk + 5 * g)

        lam_init = 0.8 - 0.6 * math.exp(-0.3 * l)
        y_b = _diff_attn(slopes_diff, jnp.full((1,), lam_init, F32), mix, diff_lambda[l].astype(F32),
                         diff_subln[l].reshape(1, 2 * HEAD_DIM).astype(F32), t=t, heads=diff_heads, tq=tq_diff,
                         tk=tk_diff, q_blk=0, k_blk=diff_heads, v_blk=2 * diff_heads)

        tc = _tile(bw, 512)
        y_c = _sconv(mix, sc_conv[l].astype(F32), t=t, width=bw, b_blk=3 * bw // tc, tc=tc)
        row = lambda v: v.reshape(1, bw).astype(F32)
        y_d = _conformer(mix, cf_conv_w[l].astype(F32), row(cf_conv_b[l]), row(cf_ln_g[l]), row(cf_ln_b[l]),
                         t=t, width=bw, a_blk=6)

        merged = _merge((y_a, y_b, y_c, y_d), w_branch, l, mgate, t=t, d=d)
        xs = _matmul(merged, w_out_b, l, out_dtype=F32, res=xs, name="out_proj")

        h = _rmsnorm(xs, ffn_norm[l], BF16)
        up = _matmul(h, w_up, l, out_dtype=BF16, **wide, name="ffn_up")
        act = _ffn_conv(up, ffn_conv[l].astype(F32), t=t, d_ff=d_ff)
        xs = _matmul(act, w_down_b, l, out_dtype=F32, res=xs, tk=d_ff // 2, name="ffn_down")

    return _rmsnorm(xs, final_norm, F32).reshape(bsz, t, d)
```

```python
import functools
import math

import numpy as np
import jax
import jax.numpy as jnp
from jax import lax
from jax.experimental import pallas as pl
from jax.experimental.pallas import tpu as pltpu

F32 = jnp.float32
BF16 = jnp.bfloat16

HEAD_DIM = 128
NSA_GROUPS = 2
CMP_BLOCK = 32
CMP_STRIDE = 16
SEL_BLOCK = 64
SEL_BLOCK_LOG2 = 6
SEL_TOPN = 16
WINDOW = 512
SC_CONV_W = 3
CF_CONV_W = 31
FFN_CONV_W = 3
NORM_EPS = 1e-6
NEG = -1e30
LOG2E = 1.4426950408889634
LANES = 128
F32_SUBLANES = 8
BF16_SUBLANES = 16
VMEM_LIMIT_BYTES = 60 * 1024 * 1024


def _cparams(sem):
    return pltpu.CompilerParams(dimension_semantics=sem, vmem_limit_bytes=VMEM_LIMIT_BYTES)


def _tile(dim, pref):
    t = min(pref, dim)
    while dim % t:
        t //= 2
    return t


def _sigmoid(x):
    return 1.0 / (1.0 + jnp.exp(-x))


def _dot(a, b):
    return jnp.dot(a, b, preferred_element_type=F32)


def _dot_nt(a, b):
    return lax.dot_general(a, b, (((1,), (1,)), ((), ())), preferred_element_type=F32)


def _rmsnorm_kernel(x_ref, g_ref, o_ref):
    x = x_ref[...]
    y = x * lax.rsqrt(jnp.mean(x * x, axis=-1, keepdims=True) + NORM_EPS)
    o_ref[...] = (y * g_ref[...]).astype(o_ref.dtype)


def _rmsnorm(x, g, out_dtype):
    t, d = x.shape
    tm = _tile(t, 256)
    return pl.pallas_call(
        _rmsnorm_kernel,
        grid=(t // tm,),
        in_specs=[pl.BlockSpec((tm, d), lambda i: (i, 0)), pl.BlockSpec((1, d), lambda i: (0, 0))],
        out_specs=pl.BlockSpec((tm, d), lambda i: (i, 0)),
        out_shape=jax.ShapeDtypeStruct((t, d), out_dtype),
        compiler_params=_cparams(("parallel",)),
        name="rmsnorm",
    )(x, g.reshape(1, d).astype(F32))


def _mm_kernel(*refs, act, has_scale, has_res, nk, w_is_nk):
    refs = list(refs)
    a_ref, w_ref = refs.pop(0), refs.pop(0)
    scale_ref = refs.pop(0) if has_scale else None
    res_ref = refs.pop(0) if has_res else None
    o_ref = refs.pop(0)
    if w_is_nk:
        part = _dot_nt(a_ref[...], w_ref[0].astype(BF16))
    else:
        part = _dot(a_ref[...], w_ref[...].astype(BF16))

    def finish(acc):
        if has_scale:
            acc = acc * scale_ref[...]
        if act == "sigmoid":
            acc = _sigmoid(acc)
        if has_res:
            acc = acc + res_ref[...]
        o_ref[...] = acc.astype(o_ref.dtype)

    if nk == 1:
        finish(part)
    else:
        acc_ref = refs[-1]
        k = pl.program_id(2)

        @pl.when(k == 0)
        def _():
            acc_ref[...] = part

        @pl.when(k > 0)
        def _():
            acc_ref[...] += part

        @pl.when(k == nk - 1)
        def _():
            finish(acc_ref[...])


def _matmul(a, w, layer, *, out_dtype, n=None, col0=0, col_scale=None, act=None, res=None, w_is_nk=False, tm=1024,
            tn=1024, tk=4096, name="matmul"):
    m, kd = a.shape
    n = w.shape[1 if w_is_nk else 2] if n is None else n
    tm, tk = _tile(m, tm), _tile(kd, tk)
    nk = kd // tk
    if w_is_nk:
        assert w.dtype == F32 and col0 % F32_SUBLANES == 0
        tn = _tile(n, tn)
        w_spec = pl.BlockSpec((pl.Element(1), pl.Element(tn), pl.Element(tk)),
                              lambda i, j, k: (layer, pl.multiple_of(col0 + j * tn, F32_SUBLANES),
                                               pl.multiple_of(k * tk, LANES)))
    else:
        tn = _tile(math.gcd(n, col0) if col0 else n, tn)
        j0 = col0 // tn
        w_spec = pl.BlockSpec((None, tk, tn), lambda i, j, k: (layer, k, j0 + j))
    in_specs = [pl.BlockSpec((tm, tk), lambda i, j, k: (i, k)), w_spec]
    args = [a, w]
    if col_scale is not None:
        in_specs.append(pl.BlockSpec((1, tn), lambda i, j, k: (0, j)))
        args.append(col_scale)
    if res is not None:
        in_specs.append(pl.BlockSpec((tm, tn), lambda i, j, k: (i, j)))
        args.append(res)
    return pl.pallas_call(
        functools.partial(_mm_kernel, act=act, has_scale=col_scale is not None, has_res=res is not None, nk=nk,
                          w_is_nk=w_is_nk),
        grid=(m // tm, n // tn, nk),
        in_specs=in_specs,
        out_specs=pl.BlockSpec((tm, tn), lambda i, j, k: (i, j)),
        out_shape=jax.ShapeDtypeStruct((m, n), out_dtype),
        scratch_shapes=[pltpu.VMEM((tm, tn), F32)] if nk > 1 else [],
        compiler_params=_cparams(("parallel", "parallel", "arbitrary")),
        name=name,
    )(*args)


def _gelu_tanh(x):
    return 0.5 * x * (1.0 + jnp.tanh(math.sqrt(2.0 / math.pi) * (x + 0.044715 * (x * x * x))))


def _compress_kernel(r_ref, pe_ref, w1_ref, w2_ref, o_ref, *, ncp, half):
    r = r_ref[...]
    u = _dot(r, w1_ref[:half, :])
    v = _dot(r, w1_ref[half:, :])
    c = _dot(pe_ref[...], w1_ref[...])[0:1]
    pre = u + pltpu.roll(v, ncp - 1, 0) + c
    out = _dot(_gelu_tanh(pre).astype(BF16), w2_ref[...])
    row = lax.broadcasted_iota(jnp.int32, out.shape, 0)
    o_ref[...] = jnp.where(row < ncp - 1, out, 0.0).astype(o_ref.dtype)


def _compress(chunks, pe, w1, w2):
    _, g, ncp, half = chunks.shape
    return pl.pallas_call(
        functools.partial(_compress_kernel, ncp=ncp, half=half),
        grid=(2, g),
        in_specs=[pl.BlockSpec((None, None, ncp, half), lambda a, b: (a, b, 0, 0)),
                  pl.BlockSpec((None, BF16_SUBLANES, 2 * half), lambda a, b: (a, 0, 0)),
                  pl.BlockSpec((None, 2 * half, HEAD_DIM), lambda a, b: (a, 0, 0)),
                  pl.BlockSpec((None, HEAD_DIM, HEAD_DIM), lambda a, b: (a, 0, 0))],
        out_specs=pl.BlockSpec((None, None, ncp, HEAD_DIM), lambda a, b: (a, b, 0, 0)),
        out_shape=jax.ShapeDtypeStruct((2, g, ncp, HEAD_DIM), BF16),
        compiler_params=_cparams(("parallel", "parallel")),
        name="nsa_compress",
    )(chunks, pe, w1, w2)


def _cmp_attn_kernel(slopes_ref, q_ref, k_ref, v_ref, shared_ref, o_ref, imp_ref, *, tq, rep):
    g, i = pl.program_id(0), pl.program_id(1)
    kc, vc = k_ref[...], v_ref[...]
    ncp = kc.shape[0]
    tpos = i * tq + lax.broadcasted_iota(jnp.int32, (tq, ncp), 0)
    cend = lax.broadcasted_iota(jnp.int32, (tq, ncp), 1) * CMP_STRIDE + (CMP_BLOCK - 1)
    dist = tpos - cend
    vis = dist >= 0
    distf = dist.astype(F32)
    anyvis = tpos >= CMP_BLOCK - 1
    psum = jnp.zeros((tq, ncp), F32)
    logits = []
    for r in range(rep):
        slope = slopes_ref[g * rep + r]
        s = _dot_nt(q_ref[:, r * HEAD_DIM:(r + 1) * HEAD_DIM], kc) - slope * distf
        logits.append(jnp.where(vis, s, NEG))
    for r in range(rep):
        s = logits[r]
        p = jnp.exp2(s - jnp.max(s, axis=-1, keepdims=True))
        p = jnp.where(anyvis, p / jnp.sum(p, axis=-1, keepdims=True), 0.0)
        o_ref[:, r * HEAD_DIM:(r + 1) * HEAD_DIM] = _dot(p.astype(BF16), vc)
        psum = psum + p
    hi = psum.astype(BF16)
    lo = (psum - hi.astype(F32)).astype(BF16)
    imp_ref[...] = _dot(hi, shared_ref[...]) + _dot(lo, shared_ref[...])


def _cmp_attn(slopes, qkv, kv_cmp, shared, *, t, g, rep, tq):
    ncp, nselp = shared.shape
    hw = rep * HEAD_DIM
    return pl.pallas_call(
        functools.partial(_cmp_attn_kernel, tq=tq, rep=rep),
        grid=(g, t // tq),
        in_specs=[pl.BlockSpec(memory_space=pltpu.SMEM),
                  pl.BlockSpec((tq, hw), lambda a, i: (i, a)),
                  pl.BlockSpec((None, None, ncp, HEAD_DIM), lambda a, i: (0, a, 0, 0)),
                  pl.BlockSpec((None, None, ncp, HEAD_DIM), lambda a, i: (1, a, 0, 0)),
                  pl.BlockSpec((ncp, nselp), lambda a, i: (0, 0))],
        out_specs=[pl.BlockSpec((tq, hw), lambda a, i: (i, a)),
                   pl.BlockSpec((None, tq, nselp), lambda a, i: (a, i, 0))],
        out_shape=[jax.ShapeDtypeStruct((t, g * hw), F32),
                   jax.ShapeDtypeStruct((g, t, nselp), F32)],
        compiler_params=_cparams(("parallel", "parallel")),
        name="nsa_cmp_attn",
    )(slopes, qkv, kv_cmp, kv_cmp, shared)


def _topk_kernel(imp_ref, sel_ref, *, tq, top_n):
    i = pl.program_id(1)
    imp = imp_ref[...].T
    blk = lax.broadcasted_iota(jnp.int32, imp.shape, 0)
    qblk = jnp.right_shift(i * tq + lax.broadcasted_iota(jnp.int32, imp.shape, 1), SEL_BLOCK_LOG2)
    forced = (blk == 0) | (blk == qblk) | (blk == qblk - 1)
    vals = jnp.where(forced, -NEG, imp)
    vals = jnp.where(blk <= qblk, vals, NEG)
    blkf = blk.astype(F32)
    nblk = float(imp.shape[0])

    def body(_, carry):
        vals, sel = carry
        m = jnp.max(vals, axis=0, keepdims=True)
        first = jnp.min(jnp.where(vals == m, blkf, nblk), axis=0, keepdims=True)
        pick = blkf == first
        sel = jnp.where(pick & (m > 0.5 * NEG), 1.0, sel)
        return jnp.where(pick, -jnp.inf, vals), sel

    _, sel = lax.fori_loop(0, top_n, body, (vals, jnp.zeros(imp.shape, F32)))
    sel_ref[...] = sel.T.astype(sel_ref.dtype)


def _topk_mask(imp, *, top_n):
    g, t, nselp = imp.shape
    tq = _tile(t, 256)
    return pl.pallas_call(
        functools.partial(_topk_kernel, tq=tq, top_n=top_n),
        grid=(g, t // tq),
        in_specs=[pl.BlockSpec((None, tq, nselp), lambda a, i: (a, i, 0))],
        out_specs=pl.BlockSpec((None, tq, nselp), lambda a, i: (a, i, 0)),
        out_shape=jax.ShapeDtypeStruct((g, t, nselp), BF16),
        compiler_params=_cparams(("parallel", "parallel")),
        name="nsa_topk",
    )(imp)


M_INIT = 0.5 * NEG


def _flash_init(m_ref, l_ref, acc_ref):
    m_ref[...] = jnp.full(m_ref.shape, M_INIT, F32)
    l_ref[...] = jnp.zeros(l_ref.shape, F32)
    acc_ref[...] = jnp.zeros(acc_ref.shape, F32)


def _flash_update(s, vt, m_ref, l_ref, acc_ref, idx):
    tk = s.shape[1]
    m_old = m_ref[idx]
    m_new = jnp.maximum(m_old, jnp.max(s, axis=-1, keepdims=True))
    alpha = jnp.exp2(m_old - m_new)
    lsum = alpha * l_ref[idx]
    ps = []
    for j in range(tk // LANES):
        pj = jnp.exp2(s[:, j * LANES:(j + 1) * LANES] - m_new)
        lsum = lsum + pj
        ps.append(pj.astype(BF16))
    l_ref[idx] = lsum
    m_ref[idx] = m_new
    pv = _dot(jnp.concatenate(ps, axis=-1), vt)
    for c in range(pv.shape[1] // LANES):
        cols = slice(c * LANES, (c + 1) * LANES)
        acc_ref[idx, :, cols] = alpha * acc_ref[idx, :, cols] + pv[:, cols]


def _flash_result(l_ref, acc_ref, idx):
    inv = 1.0 / jnp.sum(l_ref[idx], axis=-1, keepdims=True)
    return acc_ref[idx] * inv


def _sel_attn_kernel(flags_ref, slopes_ref, q_ref, k_ref, v_ref, sel_ref, o_ref, m_ref, l_ref, acc_ref,
                     *, tq, tk, nk, rep):
    g, i = pl.program_id(0), pl.program_id(1)
    _flash_init(m_ref, l_ref, acc_ref)
    sel = sel_ref[...]
    nselp = sel.shape[1]
    blk_row = lax.broadcasted_iota(jnp.int32, (nselp, tk), 0)
    blk_col = jnp.right_shift(lax.broadcasted_iota(jnp.int32, (nselp, tk), 1), SEL_BLOCK_LOG2)
    reach = lax.broadcasted_iota(jnp.int32, (tq, tk), 0) - lax.broadcasted_iota(jnp.int32, (tq, tk), 1)
    col1 = lax.broadcasted_iota(jnp.int32, (1, tk), 1)
    flag_base = (g * pl.num_programs(1) + i) * nk

    def step(k, carry):
        @pl.when(flags_ref[flag_base + k] > 0)
        def _():
            start = pl.multiple_of(k * tk, tk)
            kt = k_ref[pl.ds(start, tk), :]
            vt = v_ref[pl.ds(start, tk), :]
            expand = (blk_row == blk_col + k * (tk // SEL_BLOCK)).astype(BF16)
            off = i * tq - k * tk
            valid = (_dot(sel, expand) > 0.5) & (reach + off >= 0)
            rel = (col1 - off).astype(F32)
            logits = []
            for r in range(rep):
                slope = slopes_ref[g * rep + r]
                s = _dot_nt(q_ref[:, r * HEAD_DIM:(r + 1) * HEAD_DIM], kt) + slope * rel
                logits.append(jnp.where(valid, s, NEG))
            for r in range(rep):
                _flash_update(logits[r], vt, m_ref, l_ref, acc_ref, r)
        return carry

    lax.fori_loop(0, (i * tq + tq - 1) // tk + 1, step, 0)
    for r in range(rep):
        o_ref[:, r * HEAD_DIM:(r + 1) * HEAD_DIM] = _flash_result(l_ref, acc_ref, r)


def _sel_attn(flags, slopes, qkv, sel, *, t, g, rep, tq, tk, k_blk, v_blk):
    nselp = sel.shape[2]
    hw = rep * HEAD_DIM
    return pl.pallas_call(
        functools.partial(_sel_attn_kernel, tq=tq, tk=tk, nk=t // tk, rep=rep),
        grid=(g, t // tq),
        in_specs=[pl.BlockSpec(memory_space=pltpu.SMEM),
                  pl.BlockSpec(memory_space=pltpu.SMEM),
                  pl.BlockSpec((tq, hw), lambda a, i: (i, a)),
                  pl.BlockSpec((t, HEAD_DIM), lambda a, i: (0, k_blk + a)),
                  pl.BlockSpec((t, HEAD_DIM), lambda a, i: (0, v_blk + a)),
                  pl.BlockSpec((None, tq, nselp), lambda a, i: (a, i, 0))],
        out_specs=pl.BlockSpec((tq, hw), lambda a, i: (i, a)),
        out_shape=jax.ShapeDtypeStruct((t, g * hw), F32),
        scratch_shapes=[pltpu.VMEM((rep, tq, LANES), F32), pltpu.VMEM((rep, tq, LANES), F32),
                        pltpu.VMEM((rep, tq, HEAD_DIM), F32)],
        compiler_params=_cparams(("parallel", "parallel")),
        name="nsa_sel_attn",
    )(flags, slopes, qkv, qkv, qkv, sel)


def _win_attn_kernel(slopes_ref, q_ref, k_ref, v_ref, gate_ref, ocmp_ref, osel_ref, o_ref, *, tq, rep):
    g, i = pl.program_id(0), pl.program_id(1)
    span = tq + WINDOW
    start = pl.multiple_of(jnp.maximum(i * tq - WINDOW, 0), tq)
    kt = k_ref[pl.ds(start, span), :]
    vt = v_ref[pl.ds(start, span), :]
    tpos = i * tq + lax.broadcasted_iota(jnp.int32, (tq, span), 0)
    kpos = start + lax.broadcasted_iota(jnp.int32, (tq, span), 1)
    dist = tpos - kpos
    valid = (dist >= 0) & (dist < WINDOW)
    distf = dist.astype(F32)
    gates = _sigmoid(gate_ref[...])
    lane = lax.broadcasted_iota(jnp.int32, gates.shape, 1)

    def gate_col(c):
        return jnp.sum(jnp.where(lane == c, gates, 0.0), axis=-1, keepdims=True)

    logits = []
    for r in range(rep):
        s = _dot_nt(q_ref[:, r * HEAD_DIM:(r + 1) * HEAD_DIM], kt) - slopes_ref[g * rep + r] * distf
        logits.append(jnp.where(valid, s, NEG))
    for r in range(rep):
        head = g * rep + r
        s = logits[r]
        p = jnp.exp2(s - jnp.max(s, axis=-1, keepdims=True))
        p = p / jnp.sum(p, axis=-1, keepdims=True)
        o_win = _dot(p.astype(BF16), vt)
        cols = slice(r * HEAD_DIM, (r + 1) * HEAD_DIM)
        o = (gate_col(3 * head) * ocmp_ref[:, cols] + gate_col(3 * head + 1) * osel_ref[:, cols]
             + gate_col(3 * head + 2) * o_win)
        o_ref[:, cols] = o.astype(o_ref.dtype)


def _win_attn(slopes, qkv, gate_logits, o_cmp, o_sel, *, t, g, rep, tq, k_blk, v_blk):
    hw = rep * HEAD_DIM
    return pl.pallas_call(
        functools.partial(_win_attn_kernel, tq=tq, rep=rep),
        grid=(g, t // tq),
        in_specs=[pl.BlockSpec(memory_space=pltpu.SMEM),
                  pl.BlockSpec((tq, hw), lambda a, i: (i, a)),
                  pl.BlockSpec((t, HEAD_DIM), lambda a, i: (0, k_blk + a)),
                  pl.BlockSpec((t, HEAD_DIM), lambda a, i: (0, v_blk + a)),
                  pl.BlockSpec((tq, LANES), lambda a, i: (i, 0)),
                  pl.BlockSpec((tq, hw), lambda a, i: (i, a)),
                  pl.BlockSpec((tq, hw), lambda a, i: (i, a))],
        out_specs=pl.BlockSpec((tq, hw), lambda a, i: (i, a)),
        out_shape=jax.ShapeDtypeStruct((t, g * hw), BF16),
        compiler_params=_cparams(("parallel", "parallel")),
        name="nsa_win_attn",
    )(slopes, qkv, qkv, qkv, gate_logits, o_cmp, o_sel)


def _diff_attn_kernel(slopes_ref, lam_init_ref, q_ref, k_ref, v_ref, lam_ref, g_ref, o_ref,
                      m_ref, l_ref, acc_ref, *, tq, tk):
    h, i = pl.program_id(0), pl.program_id(1)
    slope = slopes_ref[h]
    _flash_init(m_ref, l_ref, acc_ref)
    col1 = lax.broadcasted_iota(jnp.int32, (1, tk), 1)
    reach = lax.broadcasted_iota(jnp.int32, (tq, tk), 0) - lax.broadcasted_iota(jnp.int32, (tq, tk), 1)

    def step(k, masked):
        start = pl.multiple_of(k * tk, tk)
        kt = k_ref[pl.ds(start, tk), :]
        vt = v_ref[pl.ds(start, tk), :]
        off = i * tq - k * tk
        bias = slope * (col1 - off).astype(F32)
        logits = []
        for c in range(2):
            cols = slice(c * HEAD_DIM, (c + 1) * HEAD_DIM)
            s = _dot_nt(q_ref[:, cols], kt[:, cols]) + bias
            if masked:
                s = jnp.where(reach + off >= 0, s, NEG)
            logits.append(s)
        for c in range(2):
            _flash_update(logits[c], vt, m_ref, l_ref, acc_ref, c)

    def full_step(k, carry):
        step(k, False)
        return carry

    n_full = (i * tq) // tk
    lax.fori_loop(0, n_full, full_step, 0)
    step(n_full, True)

    lv = lam_ref[...]
    lam_init = lam_init_ref[0]
    lam = (jnp.exp(jnp.sum(lv[0:1] * lv[1:2], axis=-1, keepdims=True))
           - jnp.exp(jnp.sum(lv[2:3] * lv[3:4], axis=-1, keepdims=True)) + lam_init)
    o = _flash_result(l_ref, acc_ref, 0) - lam * _flash_result(l_ref, acc_ref, 1)
    o = o * lax.rsqrt(jnp.mean(o * o, axis=-1, keepdims=True) + NORM_EPS) * g_ref[...]
    o_ref[...] = (o * (1.0 - lam_init)).astype(o_ref.dtype)


def _diff_attn(slopes, lam_init, proj, lam_vec, subln_g, *, t, heads, tq, tk, q_blk, k_blk, v_blk):
    assert tk % tq == 0
    hw = 2 * HEAD_DIM
    return pl.pallas_call(
        functools.partial(_diff_attn_kernel, tq=tq, tk=tk),
        grid=(heads, t // tq),
        in_specs=[pl.BlockSpec(memory_space=pltpu.SMEM),
                  pl.BlockSpec(memory_space=pltpu.SMEM),
                  pl.BlockSpec((tq, hw), lambda h, i: (i, q_blk + h)),
                  pl.BlockSpec((t, hw), lambda h, i: (0, k_blk + h)),
                  pl.BlockSpec((t, hw), lambda h, i: (0, v_blk + h)),
                  pl.BlockSpec((4, HEAD_DIM), lambda h, i: (0, 0)),
                  pl.BlockSpec((1, hw), lambda h, i: (0, 0))],
        out_specs=pl.BlockSpec((tq, hw), lambda h, i: (i, h)),
        out_shape=jax.ShapeDtypeStruct((t, heads * hw), BF16),
        scratch_shapes=[pltpu.VMEM((2, tq, LANES), F32), pltpu.VMEM((2, tq, LANES), F32),
                        pltpu.VMEM((2, tq, hw), F32)],
        compiler_params=_cparams(("parallel", "parallel")),
        name="diff_attn",
    )(slopes, lam_init, proj, proj, proj, lam_vec, subln_g)


def _fill_conv_scratch(scr_ref, halo, cur, halo_rows, first_tile):
    scr_ref[0:halo_rows, :] = jnp.where(first_tile, 0.0, halo)
    scr_ref[halo_rows:, :] = cur


def _causal_taps(scr_ref, w_ref, halo_rows, rows, taps):
    acc = None
    for k in range(taps):
        term = w_ref[k:k + 1, :] * scr_ref[pl.ds(halo_rows - taps + 1 + k, rows), :]
        acc = term if acc is None else acc + term
    return acc


def _halo_spec(rows, halo_rows, width, col_blk):
    per = rows // halo_rows
    return pl.BlockSpec((halo_rows, width), lambda i, j: (jnp.maximum(i * per - 1, 0), col_blk + j))


def _sconv_kernel(b_ref, c_ref, x_ref, ch_ref, xh_ref, w_ref, o_ref, scr_ref, *, tm, halo_rows):
    first = pl.program_id(0) == 0
    halo = ch_ref[...].astype(F32) * xh_ref[...].astype(F32)
    cur = c_ref[...].astype(F32) * x_ref[...].astype(F32)
    _fill_conv_scratch(scr_ref, halo, cur, halo_rows, first)
    y = b_ref[...].astype(F32) * _causal_taps(scr_ref, w_ref, halo_rows, tm, SC_CONV_W)
    o_ref[...] = y.astype(o_ref.dtype)


def _sconv(proj, w, *, t, width, b_blk, tc):
    tm = _tile(t, 512)
    hr = BF16_SUBLANES
    nb = width // tc
    cur = lambda off: pl.BlockSpec((tm, tc), lambda i, j: (i, off + j))
    return pl.pallas_call(
        functools.partial(_sconv_kernel, tm=tm, halo_rows=hr),
        grid=(t // tm, nb),
        in_specs=[cur(b_blk), cur(b_blk + nb), cur(b_blk + 2 * nb),
                  _halo_spec(tm, hr, tc, b_blk + nb), _halo_spec(tm, hr, tc, b_blk + 2 * nb),
                  pl.BlockSpec((SC_CONV_W, tc), lambda i, j: (0, j))],
        out_specs=pl.BlockSpec((tm, tc), lambda i, j: (i, j)),
        out_shape=jax.ShapeDtypeStruct((t, width), BF16),
        scratch_shapes=[pltpu.VMEM((hr + tm, tc), F32)],
        compiler_params=_cparams(("parallel", "parallel")),
        name="short_conv",
    )(proj, proj, proj, proj, proj, w)


def _causal_taps_banked(scr_ref, bank_ref, w_ref, halo_rows, rows, taps):
    total = halo_rows + rows
    for b in range(1, F32_SUBLANES):
        bank_ref[b - 1, F32_SUBLANES:, :] = scr_ref[pl.ds(F32_SUBLANES - b, total - F32_SUBLANES), :]
    acc = None
    for k in range(taps):
        a, b = divmod(taps - 1 - k, F32_SUBLANES)
        src = scr_ref if b == 0 else bank_ref.at[b - 1]
        term = w_ref[k:k + 1, :] * src[pl.ds(halo_rows - F32_SUBLANES * a, rows), :]
        acc = term if acc is None else acc + term
    return acc


def _conformer_kernel(a_ref, g_ref, ah_ref, gh_ref, w_ref, b_ref, lg_ref, lb_ref, o_ref, scr_ref, bank_ref,
                      *, tm, halo_rows):
    first = pl.program_id(0) == 0
    halo = ah_ref[...].astype(F32) * _sigmoid(gh_ref[...].astype(F32))
    cur = a_ref[...].astype(F32) * _sigmoid(g_ref[...].astype(F32))
    _fill_conv_scratch(scr_ref, halo, cur, halo_rows, first)
    u = _causal_taps_banked(scr_ref, bank_ref, w_ref, halo_rows, tm, CF_CONV_W) + b_ref[...]
    mu = jnp.mean(u, axis=-1, keepdims=True)
    uc = u - mu
    y = uc * lax.rsqrt(jnp.mean(uc * uc, axis=-1, keepdims=True) + NORM_EPS)
    y = y * lg_ref[...] + lb_ref[...]
    o_ref[...] = (y * _sigmoid(y)).astype(o_ref.dtype)


def _conformer(proj, w, b, ln_g, ln_b, *, t, width, a_blk):
    tm = _tile(t, 256)
    hr = 32
    vec = pl.BlockSpec((1, width), lambda i, j: (0, 0))
    return pl.pallas_call(
        functools.partial(_conformer_kernel, tm=tm, halo_rows=hr),
        grid=(t // tm, 1),
        in_specs=[pl.BlockSpec((tm, width), lambda i, j: (i, a_blk)),
                  pl.BlockSpec((tm, width), lambda i, j: (i, a_blk + 1)),
                  _halo_spec(tm, hr, width, a_blk), _halo_spec(tm, hr, width, a_blk + 1),
                  pl.BlockSpec((CF_CONV_W, width), lambda i, j: (0, 0)), vec, vec, vec],
        out_specs=pl.BlockSpec((tm, width), lambda i, j: (i, 0)),
        out_shape=jax.ShapeDtypeStruct((t, width), BF16),
        scratch_shapes=[pltpu.VMEM((hr + tm, width), F32), pltpu.VMEM((F32_SUBLANES - 1, hr + tm, width), F32)],
        compiler_params=_cparams(("parallel", "arbitrary")),
        name="conformer_conv",
    )(proj, proj, proj, proj, w, b, ln_g, ln_b)


def _ffn_conv_kernel(g_ref, v_ref, gh_ref, vh_ref, wg_ref, wv_ref, o_ref, *, tm, halo_rows):
    tc = g_ref.shape[1]
    first = pl.program_id(0) == 0
    r = lax.broadcasted_iota(jnp.int32, (2 * tm, tm), 0)
    c = lax.broadcasted_iota(jnp.int32, (2 * tm, tm), 1)
    shift = (c == jnp.where(r >= tm, r - tm - 2, r - 1)).astype(BF16)
    sh = _dot(shift, jnp.concatenate([g_ref[...], v_ref[...]], axis=1))
    hrow = lax.broadcasted_iota(jnp.int32, (halo_rows, tc), 0)

    def conv(x_ref, h_ref, w_ref, x1, x2):
        y = w_ref[2:3, :] * x_ref[...].astype(F32) + w_ref[1:2, :] * x1 + w_ref[0:1, :] * x2
        h = jnp.where(first, 0.0, h_ref[...].astype(F32))
        prev1, prev2 = h[halo_rows - 1:halo_rows], h[halo_rows - 2:halo_rows - 1]
        patch = jnp.where(hrow == 0, w_ref[1:2, :] * prev1 + w_ref[0:1, :] * prev2,
                          jnp.where(hrow == 1, w_ref[0:1, :] * prev1, 0.0))
        return jnp.concatenate([y[:halo_rows] + patch, y[halo_rows:]], axis=0)

    gate = conv(g_ref, gh_ref, wg_ref, sh[:tm, :tc], sh[tm:, :tc])
    val = conv(v_ref, vh_ref, wv_ref, sh[:tm, tc:], sh[tm:, tc:])
    o_ref[...] = (gate * _sigmoid(gate) * val).astype(o_ref.dtype)


def _ffn_conv(up, w, *, t, d_ff):
    tm = _tile(t, 256)
    tc = _tile(d_ff, 1024)
    hr = BF16_SUBLANES
    nb = d_ff // tc
    return pl.pallas_call(
        functools.partial(_ffn_conv_kernel, tm=tm, halo_rows=hr),
        grid=(t // tm, nb),
        in_specs=[pl.BlockSpec((tm, tc), lambda i, j: (i, j)),
                  pl.BlockSpec((tm, tc), lambda i, j: (i, nb + j)),
                  _halo_spec(tm, hr, tc, 0), _halo_spec(tm, hr, tc, nb),
                  pl.BlockSpec((FFN_CONV_W, tc), lambda i, j: (0, j)),
                  pl.BlockSpec((FFN_CONV_W, tc), lambda i, j: (0, nb + j))],
        out_specs=pl.BlockSpec((tm, tc), lambda i, j: (i, j)),
        out_shape=jax.ShapeDtypeStruct((t, d_ff), BF16),
        compiler_params=_cparams(("parallel", "parallel")),
        name="ffn_conv",
    )(up, up, up, up, w, w)


def _merge_kernel(ya_ref, yb_ref, yc_ref, yd_ref, w_ref, ga_ref, gb_ref, gc_ref, gd_ref, o_ref):
    ys = (ya_ref, yb_ref, yc_ref, yd_ref)
    gs = (ga_ref, gb_ref, gc_ref, gd_ref)
    acc = None
    for b in range(4):
        term = gs[b][...].astype(F32) * _dot(ys[b][...], w_ref[b].astype(BF16))
        acc = term if acc is None else acc + term
    o_ref[...] = acc.astype(o_ref.dtype)


def _merge(ys, w_branch, layer, gates, *, t, d):
    width = ys[0].shape[1]
    tm, tn = _tile(t, 1024), _tile(d, 512)
    nb = d // tn
    y_spec = pl.BlockSpec((tm, width), lambda i, j: (i, 0))
    gate_spec = lambda b: pl.BlockSpec((tm, tn), lambda i, j: (i, b * nb + j))
    return pl.pallas_call(
        _merge_kernel,
        grid=(t // tm, nb),
        in_specs=[y_spec, y_spec, y_spec, y_spec,
                  pl.BlockSpec((None, 4, width, tn), lambda i, j: (layer, 0, 0, j)),
                  gate_spec(0), gate_spec(1), gate_spec(2), gate_spec(3)],
        out_specs=pl.BlockSpec((tm, tn), lambda i, j: (i, j)),
        out_shape=jax.ShapeDtypeStruct((t, d), BF16),
        compiler_params=_cparams(("parallel", "parallel")),
        name="gated_merge",
    )(*ys, w_branch, gates, gates, gates, gates)


def _alibi_slopes(nsa_heads, diff_heads):
    n = nsa_heads + diff_heads
    s = 2.0 ** (-8.0 * np.arange(1, n + 1) / n)
    stride = n // diff_heads
    diff_idx = np.arange(diff_heads) * stride + stride - 1
    nsa_idx = np.setdiff1d(np.arange(n), diff_idx)
    return jnp.asarray(s[nsa_idx], F32), jnp.asarray(s[diff_idx], F32)


def _shared_tokens(t, ncp, nselp):
    n_cmp = (t - CMP_BLOCK) // CMP_STRIDE + 1
    n_sel = t // SEL_BLOCK
    cs = np.arange(n_cmp) * CMP_STRIDE
    ss = np.arange(n_sel) * SEL_BLOCK
    sh = np.clip(np.minimum(cs[:, None] + CMP_BLOCK, ss[None, :] + SEL_BLOCK)
                 - np.maximum(cs[:, None], ss[None, :]), 0, None).astype(np.float32)
    out = np.zeros((ncp, nselp), np.float32)
    out[:n_cmp, :n_sel] = sh
    return jnp.asarray(out, BF16)


def kernel(x, attn_norm, w_in, cmp_pe, cmp_w1, cmp_w2, diff_lambda, diff_subln, sc_conv, cf_conv_w, cf_conv_b,
           cf_ln_g, cf_ln_b, w_branch, w_out, ffn_norm, w_up, ffn_conv, w_down, final_norm):
    bsz, t, d = x.shape
    assert bsz == 1
    depth = w_in.shape[0]
    bw = d // 4
    g = NSA_GROUPS
    nsa_heads = bw // HEAD_DIM
    rep = nsa_heads // g
    diff_heads = bw // (2 * HEAD_DIM)
    d_ff = w_down.shape[1]
    kv_cols = 3 * 2 * g * HEAD_DIM
    gate_cols = 3 * nsa_heads
    assert gate_cols <= LANES and t % CMP_STRIDE == 0 and t % SEL_BLOCK == 0
    ncp = t // CMP_STRIDE
    n_sel = t // SEL_BLOCK
    nselp = -(-n_sel // LANES) * LANES
    top_n = min(SEL_TOPN, n_sel)
    tq = _tile(t, 256)
    assert t >= tq + WINDOW

    slopes_nsa, slopes_diff = _alibi_slopes(nsa_heads, diff_heads)
    slopes_nsa, slopes_diff = slopes_nsa * LOG2E, slopes_diff * LOG2E
    q_factor = HEAD_DIM ** -0.5 * LOG2E
    shared = _shared_tokens(t, ncp, nselp)

    c_qkv = bw + kv_cols
    c_gate = c_qkv + gate_cols
    c_mix = c_gate + 3 * bw + 3 * bw + 2 * bw
    kv_blk = bw // HEAD_DIM

    w_in_t = jnp.swapaxes(w_in, 1, 2)
    w_tail, tail0 = (w_in_t, c_gate) if c_gate % F32_SUBLANES == 0 else (w_in_t[:, c_gate:, :], 0)
    n_mix = c_mix - c_gate
    q_cols = lambda n: jnp.where(jnp.arange(n) < bw, q_factor, 1.0).astype(F32).reshape(1, n)
    w_out_b, w_down_b = w_out.astype(BF16), w_down.astype(BF16)
    wide = dict(tm=2048, tn=512)
    cmp_w1_b, cmp_w2_b = cmp_w1.astype(BF16), cmp_w2.astype(BF16)
    tk_sel = _tile(t, 512)
    tk_diff = _tile(t, 512)
    tq_diff = _tile(t, 512)

    xs = x.reshape(t, d)
    for l in range(depth):
        h = _rmsnorm(xs, attn_norm[l], BF16)
        qkv = _matmul(h, w_in_t, l, n=c_qkv, col_scale=q_cols(c_qkv), out_dtype=BF16, w_is_nk=True, **wide,
                      name="proj_qkv")
        nsa_gate = _matmul(h, w_in_t, l, n=LANES, col0=c_qkv, out_dtype=F32, w_is_nk=True, **wide,
                           name="proj_nsa_gate")
        mix = _matmul(h, w_tail, l, n=n_mix, col0=tail0, col_scale=q_cols(n_mix), out_dtype=BF16, w_is_nk=True,
                      **wide, name="proj_mix")
        mgate = _matmul(h, w_tail, l, n=4 * d, col0=tail0 + n_mix, out_dtype=BF16, act="sigmoid", w_is_nk=True,
                        **wide, name="proj_merge_gate")

        chunks = qkv[:, bw:bw + 2 * g * HEAD_DIM].reshape(ncp, CMP_STRIDE, 2, g, HEAD_DIM)
        chunks = chunks.transpose(2, 3, 0, 1, 4).reshape(2, g, ncp, CMP_STRIDE * HEAD_DIM)
        pe = jnp.broadcast_to(cmp_pe[l].reshape(2, 1, CMP_BLOCK * HEAD_DIM),
                              (2, BF16_SUBLANES, CMP_BLOCK * HEAD_DIM)).astype(BF16)
        kv_cmp = _compress(chunks, pe, cmp_w1_b[l], cmp_w2_b[l])
        o_cmp, imp = _cmp_attn(slopes_nsa, qkv, kv_cmp, shared, t=t, g=g, rep=rep, tq=tq)
        sel = _topk_mask(imp, top_n=top_n)
        flags = sel[:, :, :n_sel].reshape(g, t // tq, tq, t // tk_sel, tk_sel // SEL_BLOCK).max(axis=(2, 4))
        flags = (flags > 0).astype(jnp.int32).reshape(-1)
        o_sel = _sel_attn(flags, slopes_nsa, qkv, sel, t=t, g=g, rep=rep, tq=tq, tk=tk_sel,
                          k_blk=kv_blk + 2 * g, v_blk=kv_blk + 3 * g)
        y_a = _win_attn(slopes_nsa, qkv, nsa_gate, o_cmp, o_sel, t=t, g=g, rep=rep, tq=tq,
                        k_blk=kv_blk + 4 * g, v_blk=kv_blk + 5 * g)

        lam_init = 0.8 - 0.6 * math.exp(-0.3 * l)
        y_b = _diff_attn(slopes_diff, jnp.full((1,), lam_init, F32), mix, diff_lambda[l].astype(F32),
                         diff_subln[l].reshape(1, 2 * HEAD_DIM).astype(F32), t=t, heads=diff_heads, tq=tq_diff,
                         tk=tk_diff, q_blk=0, k_blk=diff_heads, v_blk=2 * diff_heads)

        tc = _tile(bw, 512)
        y_c = _sconv(mix, sc_conv[l].astype(F32), t=t, width=bw, b_blk=3 * bw // tc, tc=tc)
        row = lambda v: v.reshape(1, bw).astype(F32)
        y_d = _conformer(mix, cf_conv_w[l].astype(F32), row(cf_conv_b[l]), row(cf_ln_g[l]), row(cf_ln_b[l]),
                         t=t, width=bw, a_blk=6)

        merged = _merge((y_a, y_b, y_c, y_d), w_branch, l, mgate, t=t, d=d)
        xs = _matmul(merged, w_out_b, l, out_dtype=F32, res=xs, name="out_proj")

        h = _rmsnorm(xs, ffn_norm[l], BF16)
        up = _matmul(h, w_up, l, out_dtype=BF16, **wide, name="ffn_up")
        act = _ffn_conv(up, ffn_conv[l].astype(F32), t=t, d_ff=d_ff)
        xs = _matmul(act, w_down_b, l, out_dtype=F32, res=xs, tk=d_ff // 2, name="ffn_down")

    return _rmsnorm(xs, final_norm, F32).reshape(bsz, t, d)
```
